```python
import jax
import jax.numpy as jnp
from jax import lax
import numpy as np

D_MODEL = 1024
BATCH = 1
SEQ = 16384
DEPTH = 4

EPS = 1e-6
NEG_INF = -1e30
RET_HEADS = 4
RET_DK = 128
RET_DV = 128
RET_CHUNK = 128
RET_ROPE_BASE = 10000.0
SG_GROUPS = 4
SG_DIM = 128
SG_CHUNK = 128
DIL_PAIRS = ((128, 1), (512, 4), (2048, 16))
DIL_HEADS = 4
DIL_HD = 128
DIL_BLOCK = 128
N_DIL = 3
RET_KW = RET_HEADS * RET_DK
RET_W = RET_HEADS * RET_DV
SG_W = SG_GROUPS * SG_DIM
DIL_W = DIL_HEADS * DIL_HD
BRANCH_W = 512
N_BRANCH = 3
D_FF = ((8 * D_MODEL + 3 * 256 - 1) // (3 * 256)) * 256
IN_SPLITS = (RET_KW, RET_KW, RET_W, RET_W, SG_W, SG_W,
             N_DIL * DIL_W, N_DIL * DIL_W, N_DIL * DIL_W, N_BRANCH * D_MODEL)
N_IN = 10752

kernel_name = "hybrid_retention_gmlp_dilated_attention_trunk"


def rms_norm(x, g):
    xf = x.astype(jnp.float32)
    y = xf * lax.rsqrt(jnp.mean(xf * xf, axis=-1, keepdims=True) + EPS)
    return (y * g.astype(jnp.float32)).astype(x.dtype)


def rms_normalize(x):
    xf = x.astype(jnp.float32)
    return xf * lax.rsqrt(jnp.mean(xf * xf, axis=-1, keepdims=True) + EPS)


def layer_norm(x, g, b):
    xf = x.astype(jnp.float32)
    mu = jnp.mean(xf, axis=-1, keepdims=True)
    xc = xf - mu
    var = jnp.mean(xc * xc, axis=-1, keepdims=True)
    return (xc * lax.rsqrt(var + EPS) * g.astype(jnp.float32) + b.astype(jnp.float32)).astype(x.dtype)


def split_cols(z):
    idx, acc = [], 0
    for n in IN_SPLITS[:-1]:
        acc += n
        idx.append(acc)
    return jnp.split(z, idx, axis=-1)


def rotary(x, pos):
    half = x.shape[-1] // 2
    inv = 1.0 / (RET_ROPE_BASE ** (jnp.arange(half, dtype=jnp.float32) / half))
    ang = pos.astype(jnp.float32)[:, None] * inv[None, :]
    cos = jnp.cos(ang)[None, :, None, :]
    sin = jnp.sin(ang)[None, :, None, :]
    xf = x.astype(jnp.float32)
    x1, x2 = xf[..., :half], xf[..., half:]
    return jnp.concatenate([x1 * cos - x2 * sin, x1 * sin + x2 * cos], axis=-1)


def retention(q, k, v):
    B, S, H, dk = q.shape
    dv = v.shape[-1]
    C = RET_CHUNK
    nc = S // C
    log_g = jnp.log1p(-(2.0 ** (-5.0 - jnp.arange(H, dtype=jnp.float32))))
    qc = q.reshape(B, nc, C, H, dk)
    kc = k.reshape(B, nc, C, H, dk) * (dk ** -0.5)
    vc = v.astype(jnp.float32).reshape(B, nc, C, H, dv)
    n = jnp.arange(C, dtype=jnp.float32)
    diff = n[:, None] - n[None, :]
    decay_in = jnp.where(diff >= 0, jnp.exp(log_g[:, None, None] * jnp.maximum(diff, 0.0)), 0.0)
    scores = jnp.einsum('bcnhd,bcmhd->bchnm', qc, kc) * decay_in
    inner = jnp.einsum('bchnm,bcmhe->bcnhe', scores, vc)
    zeta = jnp.exp(log_g[:, None] * (C - 1.0 - n)[None, :])
    chunk_kv = jnp.einsum('bcmhd,hm,bcmhe->bchde', kc, zeta, vc)
    g_chunk = jnp.exp(log_g * C)

    def step(R, s):
        return R * g_chunk[None, :, None, None] + s, R

    R0 = jnp.zeros((B, H, dk, dv), jnp.float32)
    _, R_prev = lax.scan(step, R0, jnp.moveaxis(chunk_kv, 1, 0))
    R_prev = jnp.moveaxis(R_prev, 0, 1)
    xi = jnp.exp(log_g[:, None] * (n + 1.0)[None, :])
    cross = jnp.einsum('bcnhd,bchde,hn->bcnhe', qc, R_prev, xi)
    return (inner + cross).reshape(B, S, H, dv)


def spatial_gating(u, v, ln_g, ln_b, w_s, b_s):
    B, S, _ = v.shape
    nc = S // SG_CHUNK
    v = layer_norm(v, ln_g, ln_b)
    vg = v.reshape(B, nc, SG_CHUNK, SG_GROUPS, SG_DIM)
    mask = jnp.tril(jnp.ones((SG_CHUNK, SG_CHUNK), dtype=bool))
    ws = jnp.where(mask[None], w_s, jnp.zeros_like(w_s))
    mixed = jnp.einsum('gnm,bcmgd->bcngd', ws, vg) + b_s.T[None, None, :, :, None]
    return u * mixed.reshape(B, S, SG_W)


def dilated_window_attention(q, k, v, window, dilation):
    B, S, H, hd = q.shape
    d = dilation
    n_back = window // d
    nprev = -(-n_back // DIL_BLOCK)
    span = d * DIL_BLOCK
    Lp = -(-S // span) * span
    L = Lp // d
    nb = L // DIL_BLOCK
    N = B * d

    def to_sub(t):
        t = jnp.pad(t.astype(jnp.float32), ((0, 0), (0, Lp - S), (0, 0), (0, 0)))
        t = t.reshape(B, L, d, H, hd).transpose(0, 2, 1, 3, 4)
        return t.reshape(N, nb, DIL_BLOCK, H, hd)

    qb, kb, vb = to_sub(q), to_sub(k), to_sub(v)

    def band(t):
        tp = jnp.pad(t, ((0, 0), (nprev, 0), (0, 0), (0, 0), (0, 0)))
        return jnp.concatenate([tp[:, j:j + nb] for j in range(nprev + 1)], axis=2)

    kw, vw = band(kb), band(vb)
    KW = (nprev + 1) * DIL_BLOCK
    i = jnp.arange(DIL_BLOCK)[:, None]
    j = jnp.arange(KW)[None, :]
    dist = i + nprev * DIL_BLOCK - j
    blk = jnp.arange(nb)[:, None, None]
    kpos = (blk - nprev) * DIL_BLOCK + j[None]
    valid = (dist >= 0)[None] & (dist <= n_back)[None] & (kpos >= 0)
    s = jnp.einsum('nbqhd,nbkhd->nbhqk', qb, kw) * (hd ** -0.5)
    s = jnp.where(valid[None, :, None], s, NEG_INF)
    m = jnp.max(s, axis=-1, keepdims=True)
    e = jnp.exp(s - m)
    den = jnp.sum(e, axis=-1, keepdims=True)
    lse = jnp.swapaxes((m + jnp.log(den))[..., 0], 2, 3)
    o = jnp.einsum('nbhqk,nbkhd->nbqhd', e, vw) / jnp.swapaxes(den[..., 0], 2, 3)[..., None]

    def from_sub(t):
        rest = t.shape[3:]
        t = t.reshape((B, d, L) + rest)
        t = jnp.moveaxis(t, 1, 2).reshape((B, Lp) + rest)
        return t[:, :S]

    return from_sub(o), from_sub(lse)


def setup_inputs(seed: int = 0) -> dict:
    key = jax.random.key(seed)
    ks = jax.random.split(key, 16)
    f32 = jnp.float32
    nrm = lambda k, shape: jax.random.normal(k, shape, f32)
    x = nrm(ks[0], (BATCH, SEQ, D_MODEL))
    g_mix = 1.0 + 0.02 * nrm(ks[1], (DEPTH, D_MODEL))
    w_in = nrm(ks[2], (DEPTH, D_MODEL, N_IN)) * D_MODEL ** -0.5
    ret_norm_g = 1.0 + 0.02 * nrm(ks[3], (DEPTH, RET_W))
    sg_ln_g = 1.0 + 0.02 * nrm(ks[4], (DEPTH, SG_W))
    sg_ln_b = 0.02 * nrm(ks[5], (DEPTH, SG_W))
    sg_w = nrm(ks[6], (DEPTH, SG_GROUPS, SG_CHUNK, SG_CHUNK)) * SG_CHUNK ** -0.5
    sg_b = 1.0 + 0.02 * nrm(ks[7], (DEPTH, SG_GROUPS, SG_CHUNK))
    dil_q_norm_g = 1.0 + 0.02 * nrm(ks[8], (DEPTH, DIL_HD))
    dil_k_norm_g = 1.0 + 0.02 * nrm(ks[9], (DEPTH, DIL_HD))
    w_branch = nrm(ks[10], (DEPTH, N_BRANCH, BRANCH_W, D_MODEL)) * BRANCH_W ** -0.5
    w_o = nrm(ks[11], (DEPTH, D_MODEL, D_MODEL)) * D_MODEL ** -0.5
    g_ffn = 1.0 + 0.02 * nrm(ks[12], (DEPTH, D_MODEL))
    w_gate_up = nrm(ks[13], (DEPTH, D_MODEL, 2 * D_FF)) * D_MODEL ** -0.5
    w_down = nrm(ks[14], (DEPTH, D_FF, D_MODEL)) * D_FF ** -0.5
    return {"x": x, "g_mix": g_mix, "w_in": w_in, "ret_norm_g": ret_norm_g,
            "sg_ln_g": sg_ln_g, "sg_ln_b": sg_ln_b, "sg_w": sg_w, "sg_b": sg_b,
            "dil_q_norm_g": dil_q_norm_g, "dil_k_norm_g": dil_k_norm_g,
            "w_branch": w_branch, "w_o": w_o, "g_ffn": g_ffn,
            "w_gate_up": w_gate_up, "w_down": w_down}


def reference(x, g_mix, w_in, ret_norm_g, sg_ln_g, sg_ln_b, sg_w, sg_b,
              dil_q_norm_g, dil_k_norm_g, w_branch, w_o, g_ffn, w_gate_up, w_down):
    B, S, D = x.shape
    pos = jnp.arange(S, dtype=jnp.int32)
    for l in range(DEPTH):
        h = rms_norm(x, g_mix[l])
        z = h @ w_in[l]
        qa, ka, va, ga, ub, vb, qc, kc, vc, gates = split_cols(z)
        qa = rotary(qa.reshape(B, S, RET_HEADS, RET_DK), pos)
        ka = rotary(ka.reshape(B, S, RET_HEADS, RET_DK), pos)
        ya = retention(qa, ka, va.reshape(B, S, RET_HEADS, RET_DV))
        ya = rms_normalize(ya) * ret_norm_g[l].reshape(RET_HEADS, RET_DV).astype(jnp.float32)
        ya = (jax.nn.silu(ga.astype(jnp.float32)) * ya.reshape(B, S, RET_W)).astype(x.dtype)
        yb = spatial_gating(jax.nn.gelu(ub), jax.nn.gelu(vb), sg_ln_g[l], sg_ln_b[l], sg_w[l], sg_b[l])
        qc = rms_normalize(qc.reshape(B, S, N_DIL, DIL_HEADS, DIL_HD)) * dil_q_norm_g[l].astype(jnp.float32)
        kc = rms_normalize(kc.reshape(B, S, N_DIL, DIL_HEADS, DIL_HD)) * dil_k_norm_g[l].astype(jnp.float32)
        vc = vc.reshape(B, S, N_DIL, DIL_HEADS, DIL_HD)
        outs, lses = [], []
        for gi, (win, dil) in enumerate(DIL_PAIRS):
            o, lse = dilated_window_attention(qc[:, :, gi], kc[:, :, gi], vc[:, :, gi], win, dil)
            outs.append(o)
            lses.append(lse)
        wts = jax.nn.softmax(jnp.stack(lses, axis=0), axis=0)
        yc = jnp.sum(wts[..., None] * jnp.stack(outs, axis=0), axis=0)
        yc = yc.reshape(B, S, DIL_W).astype(x.dtype)
        gt = jax.nn.sigmoid(gates.reshape(B, S, N_BRANCH, D).astype(jnp.float32)).astype(x.dtype)
        merged = (gt[:, :, 0] * (ya @ w_branch[l, 0])
                  + gt[:, :, 1] * (yb @ w_branch[l, 1])
                  + gt[:, :, 2] * (yc @ w_branch[l, 2]))
        x = x + merged @ w_o[l]
        h = rms_norm(x, g_ffn[l])
        gate, up = jnp.split(h @ w_gate_up[l], 2, axis=-1)
        x = x + (jax.nn.silu(gate) * up) @ w_down[l]
    return x
```

```python
import functools

import jax
import jax.numpy as jnp
from jax import lax
from jax.experimental import pallas as pl
from jax.experimental.pallas import tpu as pltpu

F32 = jnp.float32
BF16 = jnp.bfloat16

D_MODEL = 1024
EPS = 1e-6
NEG_INF = -1e30
HEADS = 4
HD = 128
CHUNK = 128
ROPE_BASE = 10000.0
DIL_PAIRS = ((128, 1), (512, 4), (2048, 16))
N_DIL = len(DIL_PAIRS)
BW = HEADS * HD
N_BRANCH = 3
D_FF = 2816
N_IN = 10752
TILE = 512
LSE_LANES = HD // HEADS

VMEM_LIMIT_V7X = 56 * 1024 * 1024

_ORIG_TILE_ORDER = (0, 1, 2, 3, 4, 5, 6, 9, 12, 7, 10, 13, 8, 11, 14, 15, 16, 17, 18, 19, 20)

TM_IN = 256
TB_MIX = 1024
TQ_ATT = 512
TM_OUT = 256


def _resident(shape):
    nd = len(shape)
    return pl.BlockSpec(shape, lambda *_: (0,) * nd, pipeline_mode=pl.Buffered(1))


def _inproj_kernel(x_ref, gmix_ref, w_ref, cos_ref, sin_ref, lng_ref, lnb_ref, qg_ref, kg_ref,
                   za_ref, a0_ref, a1_ref, a2_ref, gt_ref, h_ref, scr_ref):
    tm = x_ref.shape[0]
    x = x_ref[...]
    ms = jnp.mean(x * x, axis=-1, keepdims=True)
    h_ref[...] = (x * lax.rsqrt(ms + EPS) * gmix_ref[...]).astype(BF16)

    def proj(t):
        return jnp.dot(h_ref[...], w_ref[:, t * TILE:(t + 1) * TILE], preferred_element_type=F32)

    def head_norm(acc, gain):
        outs = []
        for hh in range(HEADS):
            a = acc[:, hh * HD:(hh + 1) * HD]
            outs.append(a * lax.rsqrt(jnp.mean(a * a, axis=-1, keepdims=True) + EPS) * gain)
        return outs

    cos = cos_ref[...]
    sin = sin_ref[...]
    for t in range(2):
        acc = proj(t)
        for hh in range(HEADS):
            a = acc[:, hh * HD:(hh + 1) * HD]
            za_ref[:, t * TILE + hh * HD:t * TILE + (hh + 1) * HD] = (
                a * cos + pltpu.roll(a, HD // 2, 1) * sin).astype(BF16)
    za_ref[:, 2 * TILE:3 * TILE] = proj(2).astype(BF16)
    za_ref[:, 3 * TILE:4 * TILE] = jax.nn.silu(proj(3)).astype(BF16)
    za_ref[:, 4 * TILE:5 * TILE] = jax.nn.gelu(proj(4)).astype(BF16)
    v = jax.nn.gelu(proj(5))
    mu = jnp.mean(v, axis=-1, keepdims=True)
    vc = v - mu
    var = jnp.mean(vc * vc, axis=-1, keepdims=True)
    za_ref[:, 5 * TILE:6 * TILE] = (vc * lax.rsqrt(var + EPS) * lng_ref[...] + lnb_ref[...]).astype(BF16)

    qgain = qg_ref[...] * (HD ** -0.5)
    kgain = kg_ref[...]
    for g, (a_ref, (_, d)) in enumerate(zip((a0_ref, a1_ref, a2_ref), DIL_PAIRS)):
        for c in range(3):
            acc = proj(6 + 3 * g + c)
            if c == 0:
                parts = head_norm(acc, qgain)
            elif c == 1:
                parts = head_norm(acc, kgain)
            else:
                parts = [acc[:, hh * HD:(hh + 1) * HD] for hh in range(HEADS)]
            for hh in range(HEADS):
                lo = c * TILE + hh * HD
                if d == 1:
                    a_ref[0, :, lo:lo + HD] = parts[hh].astype(BF16)
                else:
                    slot = (3 * (g - 1) + c) * HEADS + hh
                    scr_ref[slot] = parts[hh]
                    for r in range(d):
                        a_ref[r, :, lo:lo + HD] = scr_ref[slot, pl.ds(r, tm // d, stride=d), :].astype(BF16)

    for t in range(6):
        gt_ref[:, t * TILE:(t + 1) * TILE] = jax.nn.sigmoid(proj(15 + t)).astype(BF16)


def _inproj(x, gmix, w, cos, sin, lng, lnb, qg, kg):
    s = x.shape[0]
    tm = TM_IN
    row = lambda i: (i, 0)
    sub = lambda i: (0, i, 0)
    out_shape = [jax.ShapeDtypeStruct((s, 6 * TILE), BF16)]
    out_specs = [pl.BlockSpec((tm, 6 * TILE), row)]
    for _, d in DIL_PAIRS:
        out_shape.append(jax.ShapeDtypeStruct((d, s // d, 3 * TILE), BF16))
        out_specs.append(pl.BlockSpec((d, tm // d, 3 * TILE), sub))
    out_shape.append(jax.ShapeDtypeStruct((s, 6 * TILE), BF16))
    out_specs.append(pl.BlockSpec((tm, 6 * TILE), row))
    return pl.pallas_call(
        _inproj_kernel,
        grid=(s // tm,),
        in_specs=[
            pl.BlockSpec((tm, D_MODEL), row),
            _resident((1, D_MODEL)),
            _resident((D_MODEL, N_IN)),
            pl.BlockSpec((tm, HD), row),
            pl.BlockSpec((tm, HD), row),
            _resident((1, BW)),
            _resident((1, BW)),
            _resident((1, HD)),
            _resident((1, HD)),
        ],
        out_specs=out_specs,
        out_shape=out_shape,
        scratch_shapes=[pltpu.VMEM((tm, D_MODEL), BF16), pltpu.VMEM((6 * HEADS, tm, HD), F32)],
        compiler_params=pltpu.CompilerParams(
            dimension_semantics=("arbitrary",), vmem_limit_bytes=VMEM_LIMIT_V7X),
        name="inproj",
    )(x, gmix, w, cos, sin, lng, lnb, qg, kg)


def _mixers_kernel(q_ref, k_ref, v_ref, sg_ref, u_ref, vln_ref, decay_ref, zeta_ref, xi_ref,
                   gch_ref, rg_ref, ws_ref, bs_ref, ya_ref, yb_ref, state_ref):
    n_chunks = q_ref.shape[0] // CHUNK

    @pl.when(pl.program_id(0) == 0)
    def _():
        state_ref[...] = jnp.zeros_like(state_ref)

    row = lax.broadcasted_iota(jnp.int32, (CHUNK, CHUNK), 0)
    col = lax.broadcasted_iota(jnp.int32, (CHUNK, CHUNK), 1)

    for hh in range(HEADS):
        cols = slice(hh * HD, (hh + 1) * HD)
        decay = decay_ref[hh]
        zeta = zeta_ref[hh]
        xi = xi_ref[hh]
        gch = gch_ref[hh]
        gain = rg_ref[:, cols]

        def ret_chunk(c, state, cols=cols, decay=decay, zeta=zeta, xi=xi, gch=gch, gain=gain):
            rows = pl.ds(pl.multiple_of(c * CHUNK, CHUNK), CHUNK)
            q = q_ref[rows, cols]
            k = k_ref[rows, cols]
            v = v_ref[rows, cols]
            scores = lax.dot_general(q, k, (((1,), (1,)), ((), ())), preferred_element_type=F32) * decay
            inner = jnp.dot(scores.astype(BF16), v, preferred_element_type=F32)
            cross = jnp.dot(q, state.astype(BF16), preferred_element_type=F32) * xi
            vz = (v.astype(F32) * zeta).astype(BF16)
            kv = lax.dot_general(k, vz, (((0,), (0,)), ((), ())), preferred_element_type=F32)
            y = inner + cross
            y = y * lax.rsqrt(jnp.mean(y * y, axis=-1, keepdims=True) + EPS) * gain
            ya_ref[rows, cols] = (sg_ref[rows, cols].astype(F32) * y).astype(BF16)
            return state * gch + kv

        state_ref[hh] = lax.fori_loop(0, n_chunks, ret_chunk, state_ref[hh])

    for g in range(HEADS):
        cols = slice(g * HD, (g + 1) * HD)
        w = jnp.where(row >= col, ws_ref[g], 0.0).astype(BF16)
        bias = bs_ref[g]

        def sg_chunk(c, carry, cols=cols, w=w, bias=bias):
            rows = pl.ds(pl.multiple_of(c * CHUNK, CHUNK), CHUNK)
            mixed = jnp.dot(w, vln_ref[rows, cols], preferred_element_type=F32) + bias
            yb_ref[rows, cols] = (u_ref[rows, cols].astype(F32) * mixed).astype(BF16)
            return carry

        lax.fori_loop(0, n_chunks, sg_chunk, 0)


def _mixers(za, decay, zeta, xi, gch, rg, ws, bs):
    s = za.shape[0]
    tb = TB_MIX
    col_block = lambda j: pl.BlockSpec((tb, TILE), lambda i, j=j: (i, j))
    table = _resident((HEADS, CHUNK, CHUNK))
    return pl.pallas_call(
        _mixers_kernel,
        grid=(s // tb,),
        in_specs=[col_block(j) for j in range(6)] + [table, table, table, table,
                                                     _resident((1, BW)), table, table],
        out_specs=[pl.BlockSpec((tb, BW), lambda i: (i, 0))] * 2,
        out_shape=[jax.ShapeDtypeStruct((s, BW), BF16)] * 2,
        scratch_shapes=[pltpu.VMEM((HEADS, HD, HD), F32)],
        compiler_params=pltpu.CompilerParams(
            dimension_semantics=("arbitrary",), vmem_limit_bytes=VMEM_LIMIT_V7X),
        name="mixers",
    )(za, za, za, za, za, za, decay, zeta, xi, gch, rg, ws, bs)


def _attn_kernel(q_ref, kp_ref, kc_ref, vp_ref, vc_ref, o_ref, lse_ref, kbuf_ref, vbuf_ref):
    tq = q_ref.shape[0]
    first = pl.program_id(1) == 0
    kbuf_ref[0:CHUNK, :] = kp_ref[...]
    kbuf_ref[CHUNK:, :] = kc_ref[...]
    vbuf_ref[0:CHUNK, :] = vp_ref[...]
    vbuf_ref[CHUNK:, :] = vc_ref[...]

    row = lax.broadcasted_iota(jnp.int32, (CHUNK, 2 * CHUNK), 0)
    col = lax.broadcasted_iota(jnp.int32, (CHUNK, 2 * CHUNK), 1)
    band = (col >= row) & (col <= row + CHUNK)
    band_first = band & (col >= jnp.where(first, CHUNK, 0))
    lane = lax.broadcasted_iota(jnp.int32, (CHUNK, HD), 1)

    for i in range(tq // CHUNK):
        rows = slice(i * CHUNK, (i + 1) * CHUNK)
        keys = slice(i * CHUNK, (i + 2) * CHUNK)
        valid = band_first if i == 0 else band
        lse_tile = jnp.zeros((CHUNK, HD), F32)
        for hh in range(HEADS):
            cols = slice(hh * HD, (hh + 1) * HD)
            s = lax.dot_general(q_ref[rows, cols], kbuf_ref[keys, cols], (((1,), (1,)), ((), ())),
                                preferred_element_type=F32)
            s = jnp.where(valid, s, NEG_INF)
            m = jnp.max(s, axis=-1, keepdims=True)
            e = jnp.exp(s - m)
            den = jnp.sum(e, axis=-1, keepdims=True)
            o = jnp.dot(e.astype(BF16), vbuf_ref[keys, cols], preferred_element_type=F32) / den
            o_ref[rows, cols] = o.astype(BF16)
            lse_tile = jnp.where(lane // LSE_LANES == hh, m + jnp.log(den), lse_tile)
        lse_ref[rows, :] = lse_tile


def _attention(a, d):
    length = a.shape[1]
    tq = TQ_ATT
    assert length % tq == 0
    per = tq // CHUNK
    cur = lambda c: pl.BlockSpec((None, tq, BW), lambda r, b, c=c: (r, b, c))
    prev = lambda c: pl.BlockSpec((None, CHUNK, BW),
                                  lambda r, b, c=c: (r, jnp.maximum(b * per - 1, 0), c))
    return pl.pallas_call(
        _attn_kernel,
        grid=(d, length // tq),
        in_specs=[cur(0), prev(1), cur(1), prev(2), cur(2)],
        out_specs=[pl.BlockSpec((None, tq, BW), lambda r, b: (r, b, 0)),
                   pl.BlockSpec((None, tq, HD), lambda r, b: (r, b, 0))],
        out_shape=[jax.ShapeDtypeStruct((d, length, BW), BF16),
                   jax.ShapeDtypeStruct((d, length, HD), F32)],
        scratch_shapes=[pltpu.VMEM((tq + CHUNK, BW), BF16), pltpu.VMEM((tq + CHUNK, BW), BF16)],
        compiler_params=pltpu.CompilerParams(
            dimension_semantics=("arbitrary", "arbitrary"), vmem_limit_bytes=VMEM_LIMIT_V7X),
        name=f"attn_d{d}",
    )(a, a, a, a, a)


def _merge_ffn_kernel(x_ref, ya_ref, yb_ref, o0_ref, o1_ref, o2_ref, l0_ref, l1_ref, l2_ref, gt_ref,
                      wb_ref, wo_ref, gffn_ref, wgu_ref, wd_ref, out_ref, oscr_ref, lscr_ref, yc_ref):
    tm = x_ref.shape[0]

    def natural_order(src_ref, cols, scr_ref, slot, d):
        if d == 1:
            return src_ref[0, :, cols].astype(F32)
        for r in range(d):
            scr_ref[slot, pl.ds(r, tm // d, stride=d), :] = src_ref[r, :, cols].astype(F32)
        return scr_ref[slot]

    outs, lses = [], []
    for g, (o_ref, l_ref, (_, d)) in enumerate(zip((o0_ref, o1_ref, o2_ref), (l0_ref, l1_ref, l2_ref),
                                                   DIL_PAIRS)):
        outs.append([natural_order(o_ref, slice(hh * HD, (hh + 1) * HD), oscr_ref, g * HEADS + hh, d)
                     for hh in range(HEADS)])
        lses.append(natural_order(l_ref, slice(0, HD), lscr_ref, g, d))
    m = jnp.maximum(jnp.maximum(lses[0], lses[1]), lses[2])
    es = [jnp.exp(l - m) for l in lses]
    inv = 1.0 / (es[0] + es[1] + es[2])
    for hh in range(HEADS):
        cols = slice(hh * HD, (hh + 1) * HD)
        acc = jnp.zeros((tm, HD), F32)
        for g in range(N_DIL):
            wgt = (es[g] * inv)[:, hh * LSE_LANES:hh * LSE_LANES + 1]
            acc = acc + wgt * outs[g][hh]
        yc_ref[:, cols] = acc.astype(BF16)

    merged = jnp.zeros((tm, D_MODEL), F32)
    for b, y in enumerate((ya_ref[...], yb_ref[...], yc_ref[...])):
        gate = gt_ref[:, b * D_MODEL:(b + 1) * D_MODEL].astype(F32)
        merged = merged + gate * jnp.dot(y, wb_ref[b], preferred_element_type=F32)
    x1 = x_ref[...] + jnp.dot(merged.astype(BF16), wo_ref[...], preferred_element_type=F32)

    ms = jnp.mean(x1 * x1, axis=-1, keepdims=True)
    h = (x1 * lax.rsqrt(ms + EPS) * gffn_ref[...]).astype(BF16)
    gate = jnp.dot(h, wgu_ref[:, :D_FF], preferred_element_type=F32)
    up = jnp.dot(h, wgu_ref[:, D_FF:], preferred_element_type=F32)
    act = (jax.nn.silu(gate) * up).astype(BF16)
    out_ref[...] = x1 + jnp.dot(act, wd_ref[...], preferred_element_type=F32)


def _merge_ffn(x, ya, yb, outs, lses, gates, wb, wo, gffn, wgu, wd):
    s = x.shape[0]
    tm = TM_OUT
    row = lambda i: (i, 0)
    sub = lambda i: (0, i, 0)
    o_specs = [pl.BlockSpec((d, tm // d, BW), sub) for _, d in DIL_PAIRS]
    l_specs = [pl.BlockSpec((d, tm // d, HD), sub) for _, d in DIL_PAIRS]
    return pl.pallas_call(
        _merge_ffn_kernel,
        grid=(s // tm,),
        in_specs=[pl.BlockSpec((tm, D_MODEL), row),
                  pl.BlockSpec((tm, BW), row), pl.BlockSpec((tm, BW), row)]
                 + o_specs + l_specs
                 + [pl.BlockSpec((tm, N_BRANCH * D_MODEL), row),
                    _resident((N_BRANCH, BW, D_MODEL)),
                    _resident((D_MODEL, D_MODEL)),
                    _resident((1, D_MODEL)),
                    _resident((D_MODEL, 2 * D_FF)),
                    _resident((D_FF, D_MODEL))],
        out_specs=pl.BlockSpec((tm, D_MODEL), row),
        out_shape=jax.ShapeDtypeStruct((s, D_MODEL), F32),
        scratch_shapes=[pltpu.VMEM((N_DIL * HEADS, tm, HD), F32), pltpu.VMEM((N_DIL, tm, HD), F32),
                        pltpu.VMEM((tm, BW), BF16)],
        compiler_params=pltpu.CompilerParams(
            dimension_semantics=("arbitrary",), vmem_limit_bytes=VMEM_LIMIT_V7X),
        name="merge_ffn",
    )(x, ya, yb, *outs, *lses, gates, wb, wo, gffn, wgu, wd)


def _retention_tables():
    c = CHUNK
    log_g = jnp.log1p(-(2.0 ** (-5.0 - jnp.arange(HEADS, dtype=F32))))
    n = jnp.arange(c, dtype=F32)
    diff = n[:, None] - n[None, :]
    scale = HD ** -0.5
    decay = jnp.where(diff >= 0, jnp.exp(log_g[:, None, None] * jnp.maximum(diff, 0.0)), 0.0) * scale
    zeta = jnp.exp(log_g[:, None] * (c - 1.0 - n)[None, :]) * scale
    xi = jnp.exp(log_g[:, None] * (n + 1.0)[None, :])
    g_chunk = jnp.exp(log_g * c)
    full = lambda t: jnp.broadcast_to(t, (HEADS, c, c)).astype(F32)
    return decay, full(zeta[:, :, None]), full(xi[:, :, None]), full(g_chunk[:, None, None])


def _rotary_tables(s):
    half = HD // 2
    inv = 1.0 / (ROPE_BASE ** (jnp.arange(half, dtype=F32) / half))
    ang = jnp.arange(s, dtype=jnp.int32).astype(F32)[:, None] * inv[None, :]
    cos, sin = jnp.cos(ang), jnp.sin(ang)
    return jnp.concatenate([cos, cos], axis=-1), jnp.concatenate([-sin, sin], axis=-1)


def kernel(x, g_mix, w_in, ret_norm_g, sg_ln_g, sg_ln_b, sg_w, sg_b, dil_q_norm_g, dil_k_norm_g,
           w_branch, w_o, g_ffn, w_gate_up, w_down):
    batch, s, d_model = x.shape
    depth = w_in.shape[0]
    assert batch == 1 and d_model == D_MODEL and w_in.shape[-1] == N_IN
    assert s % (DIL_PAIRS[-1][1] * CHUNK) == 0 and s % TB_MIX == 0
    for win, d in DIL_PAIRS:
        assert win // d == CHUNK and (s // d) % TQ_ATT == 0 and TM_IN % (16 * d) == 0

    w_in_b = jnp.concatenate([w_in[:, :, t * TILE:(t + 1) * TILE] for t in _ORIG_TILE_ORDER],
                             axis=-1).astype(BF16)
    w_branch_b = w_branch.astype(BF16)
    w_o_b = w_o.astype(BF16)
    w_gu_b = w_gate_up.astype(BF16)
    w_d_b = w_down.astype(BF16)
    cos, sin = _rotary_tables(s)
    decay, zeta, xi, gch = _retention_tables()
    bias_s = jnp.broadcast_to(sg_b[:, :, :, None], sg_b.shape + (CHUNK,)).astype(F32)

    xs = x[0]
    for l in range(depth):
        za, a0, a1, a2, gates = _inproj(
            xs, g_mix[l][None], w_in_b[l], cos, sin, sg_ln_g[l][None], sg_ln_b[l][None],
            dil_q_norm_g[l][None], dil_k_norm_g[l][None])
        ya, yb = _mixers(za, decay, zeta, xi, gch, ret_norm_g[l][None], sg_w[l], bias_s[l])
        outs, lses = [], []
        for a, (_, d) in zip((a0, a1, a2), DIL_PAIRS):
            o, lse = _attention(a, d)
            outs.append(o)
            lses.append(lse)
        xs = _merge_ffn(xs, ya, yb, outs, lses, gates, w_branch_b[l], w_o_b[l], g_ffn[l][None],
                        w_gu_b[l], w_d_b[l])
    return xs[None]
```

```python
import functools

import jax
import jax.numpy as jnp
from jax import lax
from jax.experimental import pallas as pl
from jax.experimental.pallas import tpu as pltpu

F32 = jnp.float32
BF16 = jnp.bfloat16

D_MODEL = 1024
EPS = 1e-6
NEG_INF = -1e30
HEADS = 4
HD = 128
CHUNK = 128
ROPE_BASE = 10000.0
DIL_PAIRS = ((128, 1), (512, 4), (2048, 16))
N_DIL = len(DIL_PAIRS)
BW = HEADS * HD
N_BRANCH = 3
D_FF = 2816
N_IN = 10752
TILE = 512
LSE_LANES = HD // HEADS

VMEM_LIMIT_V7X = 56 * 1024 * 1024

_ORIG_TILE_ORDER = (0, 1, 2, 3, 4, 5, 6, 9, 12, 7, 10, 13, 8, 11, 14, 15, 16, 17, 18, 19, 20)

TM_IN = 256
TB_MIX = 512
TQ_ATT = 512
TM_OUT = 512


def _resident(shape):
    nd = len(shape)
    return pl.BlockSpec(shape, lambda *_: (0,) * nd, pipeline_mode=pl.Buffered(1))


def _layer(shape, l):
    nd = len(shape)
    return pl.BlockSpec((None,) + tuple(shape), lambda *_: (l,) + (0,) * nd,
                        pipeline_mode=pl.Buffered(1))


def _inproj_kernel(x_ref, gmix_ref, w_ref, cos_ref, sin_ref, lng_ref, lnb_ref, qg_ref, kg_ref,
                   za_ref, a0_ref, a1_ref, a2_ref, gt_ref, h_ref, scr_ref):
    tm = x_ref.shape[0]
    x = x_ref[...]
    ms = jnp.mean(x * x, axis=-1, keepdims=True)
    h_ref[...] = (x * lax.rsqrt(ms + EPS) * gmix_ref[...]).astype(BF16)

    def proj(t):
        return jnp.dot(h_ref[...], w_ref[:, t * TILE:(t + 1) * TILE], preferred_element_type=F32)

    def head_norm(acc, gain):
        outs = []
        for hh in range(HEADS):
            a = acc[:, hh * HD:(hh + 1) * HD]
            outs.append(a * lax.rsqrt(jnp.mean(a * a, axis=-1, keepdims=True) + EPS) * gain)
        return outs

    cos = cos_ref[...]
    sin = sin_ref[...]
    for t in range(2):
        acc = proj(t)
        for hh in range(HEADS):
            a = acc[:, hh * HD:(hh + 1) * HD]
            za_ref[:, t * TILE + hh * HD:t * TILE + (hh + 1) * HD] = (
                a * cos + pltpu.roll(a, HD // 2, 1) * sin).astype(BF16)
    za_ref[:, 2 * TILE:3 * TILE] = proj(2).astype(BF16)
    za_ref[:, 3 * TILE:4 * TILE] = jax.nn.silu(proj(3)).astype(BF16)
    za_ref[:, 4 * TILE:5 * TILE] = jax.nn.gelu(proj(4)).astype(BF16)
    v = jax.nn.gelu(proj(5))
    mu = jnp.mean(v, axis=-1, keepdims=True)
    vc = v - mu
    var = jnp.mean(vc * vc, axis=-1, keepdims=True)
    za_ref[:, 5 * TILE:6 * TILE] = (vc * lax.rsqrt(var + EPS) * lng_ref[...] + lnb_ref[...]).astype(BF16)

    qgain = qg_ref[...] * (HD ** -0.5)
    kgain = kg_ref[...]
    for g, (a_ref, (_, d)) in enumerate(zip((a0_ref, a1_ref, a2_ref), DIL_PAIRS)):
        for c in range(3):
            acc = proj(6 + 3 * g + c)
            if c == 0:
                parts = head_norm(acc, qgain)
            elif c == 1:
                parts = head_norm(acc, kgain)
            else:
                parts = [acc[:, hh * HD:(hh + 1) * HD] for hh in range(HEADS)]
            for hh in range(HEADS):
                lo = c * TILE + hh * HD
                if d == 1:
                    a_ref[0, :, lo:lo + HD] = parts[hh].astype(BF16)
                else:
                    slot = (3 * (g - 1) + c) * HEADS + hh
                    scr_ref[slot] = parts[hh]
                    for r in range(d):
                        a_ref[r, :, lo:lo + HD] = scr_ref[slot, pl.ds(r, tm // d, stride=d), :].astype(BF16)

    for t in range(6):
        gt_ref[:, t * TILE:(t + 1) * TILE] = jax.nn.sigmoid(proj(15 + t)).astype(BF16)


def _inproj(l, x, gmix, w, cos, sin, lng, lnb, qg, kg):
    s = x.shape[0]
    tm = TM_IN
    row = lambda i: (i, 0)
    sub = lambda i: (0, i, 0)
    out_shape = [jax.ShapeDtypeStruct((s, 6 * TILE), BF16)]
    out_specs = [pl.BlockSpec((tm, 6 * TILE), row)]
    for _, d in DIL_PAIRS:
        out_shape.append(jax.ShapeDtypeStruct((d, s // d, 3 * TILE), BF16))
        out_specs.append(pl.BlockSpec((d, tm // d, 3 * TILE), sub))
    out_shape.append(jax.ShapeDtypeStruct((s, 6 * TILE), BF16))
    out_specs.append(pl.BlockSpec((tm, 6 * TILE), row))
    return pl.pallas_call(
        _inproj_kernel,
        grid=(s // tm,),
        in_specs=[
            pl.BlockSpec((tm, D_MODEL), row),
            _layer((1, D_MODEL), l),
            _layer((D_MODEL, N_IN), l),
            pl.BlockSpec((tm, HD), row),
            pl.BlockSpec((tm, HD), row),
            _layer((1, BW), l),
            _layer((1, BW), l),
            _layer((1, HD), l),
            _layer((1, HD), l),
        ],
        out_specs=out_specs,
        out_shape=out_shape,
        scratch_shapes=[pltpu.VMEM((tm, D_MODEL), BF16), pltpu.VMEM((6 * HEADS, tm, HD), F32)],
        compiler_params=pltpu.CompilerParams(
            dimension_semantics=("arbitrary",), vmem_limit_bytes=VMEM_LIMIT_V7X),
        name="inproj",
    )(x, gmix, w, cos, sin, lng, lnb, qg, kg)


def _mixers_kernel(q_ref, k_ref, v_ref, sg_ref, u_ref, vln_ref, decay_ref, zeta_ref, xi_ref,
                   gch_ref, rg_ref, ws_ref, bs_ref, ya_ref, yb_ref, state_ref):
    n_chunks = q_ref.shape[0] // CHUNK

    @pl.when(pl.program_id(0) == 0)
    def _():
        state_ref[...] = jnp.zeros_like(state_ref)

    row = lax.broadcasted_iota(jnp.int32, (CHUNK, CHUNK), 0)
    col = lax.broadcasted_iota(jnp.int32, (CHUNK, CHUNK), 1)

    for hh in range(HEADS):
        cols = slice(hh * HD, (hh + 1) * HD)
        decay = decay_ref[hh]
        zeta = zeta_ref[hh]
        xi = xi_ref[hh]
        gch = gch_ref[hh]
        gain = rg_ref[:, cols]
        inners, kvs = [], []
        for c in range(n_chunks):
            rows = slice(c * CHUNK, (c + 1) * CHUNK)
            q = q_ref[rows, cols]
            k = k_ref[rows, cols]
            v = v_ref[rows, cols]
            scores = lax.dot_general(q, k, (((1,), (1,)), ((), ())), preferred_element_type=F32) * decay
            inners.append(jnp.dot(scores.astype(BF16), v, preferred_element_type=F32))
            vz = (v.astype(F32) * zeta).astype(BF16)
            kvs.append(lax.dot_general(k, vz, (((0,), (0,)), ((), ())), preferred_element_type=F32))
        state = state_ref[hh]
        for c in range(n_chunks):
            rows = slice(c * CHUNK, (c + 1) * CHUNK)
            cross = jnp.dot(q_ref[rows, cols], state.astype(BF16), preferred_element_type=F32) * xi
            y = inners[c] + cross
            y = y * lax.rsqrt(jnp.mean(y * y, axis=-1, keepdims=True) + EPS) * gain
            ya_ref[rows, cols] = (sg_ref[rows, cols].astype(F32) * y).astype(BF16)
            state = state * gch + kvs[c]
        state_ref[hh] = state

    for g in range(HEADS):
        cols = slice(g * HD, (g + 1) * HD)
        w = jnp.where(row >= col, ws_ref[g], 0.0).astype(BF16)
        bias = bs_ref[g]
        for c in range(n_chunks):
            rows = slice(c * CHUNK, (c + 1) * CHUNK)
            mixed = jnp.dot(w, vln_ref[rows, cols], preferred_element_type=F32) + bias
            yb_ref[rows, cols] = (u_ref[rows, cols].astype(F32) * mixed).astype(BF16)


def _mixers(l, za, decay, zeta, xi, gch, rg, ws, bs):
    s = za.shape[0]
    tb = TB_MIX
    col_block = lambda j: pl.BlockSpec((tb, TILE), lambda i, j=j: (i, j))
    table = _resident((HEADS, CHUNK, CHUNK))
    layer_table = _layer((HEADS, CHUNK, CHUNK), l)
    return pl.pallas_call(
        _mixers_kernel,
        grid=(s // tb,),
        in_specs=[col_block(j) for j in range(6)] + [table, table, table, table,
                                                     _layer((1, BW), l), layer_table, layer_table],
        out_specs=[pl.BlockSpec((tb, BW), lambda i: (i, 0))] * 2,
        out_shape=[jax.ShapeDtypeStruct((s, BW), BF16)] * 2,
        scratch_shapes=[pltpu.VMEM((HEADS, HD, HD), F32)],
        compiler_params=pltpu.CompilerParams(
            dimension_semantics=("arbitrary",), vmem_limit_bytes=VMEM_LIMIT_V7X),
        name="mixers",
    )(za, za, za, za, za, za, decay, zeta, xi, gch, rg, ws, bs)


def _attn_kernel(q_ref, kp_ref, kc_ref, vp_ref, vc_ref, o_ref, lse_ref, kbuf_ref, vbuf_ref):
    tq = q_ref.shape[0]
    first = pl.program_id(1) == 0
    kbuf_ref[0:CHUNK, :] = kp_ref[...]
    kbuf_ref[CHUNK:, :] = kc_ref[...]
    vbuf_ref[0:CHUNK, :] = vp_ref[...]
    vbuf_ref[CHUNK:, :] = vc_ref[...]

    row = lax.broadcasted_iota(jnp.int32, (CHUNK, 2 * CHUNK), 0)
    col = lax.broadcasted_iota(jnp.int32, (CHUNK, 2 * CHUNK), 1)
    band = (col >= row) & (col <= row + CHUNK)
    band_first = band & (col >= jnp.where(first, CHUNK, 0))
    lane = lax.broadcasted_iota(jnp.int32, (CHUNK, HD), 1)

    for i in range(tq // CHUNK):
        rows = slice(i * CHUNK, (i + 1) * CHUNK)
        keys = slice(i * CHUNK, (i + 2) * CHUNK)
        valid = band_first if i == 0 else band
        lse_tile = jnp.zeros((CHUNK, HD), F32)
        for hh in range(HEADS):
            cols = slice(hh * HD, (hh + 1) * HD)
            s = lax.dot_general(q_ref[rows, cols], kbuf_ref[keys, cols], (((1,), (1,)), ((), ())),
                                preferred_element_type=F32)
            s = jnp.where(valid, s, NEG_INF)
            m = jnp.max(s, axis=-1, keepdims=True)
            e = jnp.exp(s - m)
            den = jnp.sum(e, axis=-1, keepdims=True)
            o = jnp.dot(e.astype(BF16), vbuf_ref[keys, cols], preferred_element_type=F32) / den
            o_ref[rows, cols] = o.astype(BF16)
            lse_tile = jnp.where(lane // LSE_LANES == hh, m + jnp.log(den), lse_tile)
        lse_ref[rows, :] = lse_tile


def _attention(a, d):
    length = a.shape[1]
    tq = TQ_ATT
    assert length % tq == 0
    per = tq // CHUNK
    cur = lambda c: pl.BlockSpec((None, tq, BW), lambda r, b, c=c: (r, b, c))
    prev = lambda c: pl.BlockSpec((None, CHUNK, BW),
                                  lambda r, b, c=c: (r, jnp.maximum(b * per - 1, 0), c))
    return pl.pallas_call(
        _attn_kernel,
        grid=(d, length // tq),
        in_specs=[cur(0), prev(1), cur(1), prev(2), cur(2)],
        out_specs=[pl.BlockSpec((None, tq, BW), lambda r, b: (r, b, 0)),
                   pl.BlockSpec((None, tq, HD), lambda r, b: (r, b, 0))],
        out_shape=[jax.ShapeDtypeStruct((d, length, BW), BF16),
                   jax.ShapeDtypeStruct((d, length, HD), F32)],
        scratch_shapes=[pltpu.VMEM((tq + CHUNK, BW), BF16), pltpu.VMEM((tq + CHUNK, BW), BF16)],
        compiler_params=pltpu.CompilerParams(
            dimension_semantics=("arbitrary", "arbitrary"), vmem_limit_bytes=VMEM_LIMIT_V7X),
        name=f"attn_d{d}",
    )(a, a, a, a, a)


def _merge_ffn_kernel(x_ref, ya_ref, yb_ref, o0_ref, o1_ref, o2_ref, l0_ref, l1_ref, l2_ref, gt_ref,
                      wb_ref, wo_ref, gffn_ref, wgu_ref, wd_ref, out_ref, oscr_ref, lscr_ref, yc_ref):
    tm = x_ref.shape[0]

    def natural_order(src_ref, cols, scr_ref, slot, d):
        if d == 1:
            return src_ref[0, :, cols].astype(F32)
        for r in range(d):
            scr_ref[slot, pl.ds(r, tm // d, stride=d), :] = src_ref[r, :, cols].astype(F32)
        return scr_ref[slot]

    outs, lses = [], []
    for g, (o_ref, l_ref, (_, d)) in enumerate(zip((o0_ref, o1_ref, o2_ref), (l0_ref, l1_ref, l2_ref),
                                                   DIL_PAIRS)):
        outs.append([natural_order(o_ref, slice(hh * HD, (hh + 1) * HD), oscr_ref, g * HEADS + hh, d)
                     for hh in range(HEADS)])
        lses.append(natural_order(l_ref, slice(0, HD), lscr_ref, g, d))
    m = jnp.maximum(jnp.maximum(lses[0], lses[1]), lses[2])
    es = [jnp.exp(l - m) for l in lses]
    inv = 1.0 / (es[0] + es[1] + es[2])
    for hh in range(HEADS):
        cols = slice(hh * HD, (hh + 1) * HD)
        acc = jnp.zeros((tm, HD), F32)
        for g in range(N_DIL):
            wgt = (es[g] * inv)[:, hh * LSE_LANES:hh * LSE_LANES + 1]
            acc = acc + wgt * outs[g][hh]
        yc_ref[:, cols] = acc.astype(BF16)

    merged = jnp.zeros((tm, D_MODEL), F32)
    for b, y in enumerate((ya_ref[...], yb_ref[...], yc_ref[...])):
        gate = gt_ref[:, b * D_MODEL:(b + 1) * D_MODEL].astype(F32)
        merged = merged + gate * jnp.dot(y, wb_ref[b], preferred_element_type=F32)
    x1 = x_ref[...] + jnp.dot(merged.astype(BF16), wo_ref[...], preferred_element_type=F32)

    ms = jnp.mean(x1 * x1, axis=-1, keepdims=True)
    h = (x1 * lax.rsqrt(ms + EPS) * gffn_ref[...]).astype(BF16)
    gate = jnp.dot(h, wgu_ref[:, :D_FF], preferred_element_type=F32)
    up = jnp.dot(h, wgu_ref[:, D_FF:], preferred_element_type=F32)
    act = (jax.nn.silu(gate) * up).astype(BF16)
    out_ref[...] = x1 + jnp.dot(act, wd_ref[...], preferred_element_type=F32)


def _merge_ffn(l, x, ya, yb, outs, lses, gates, wb, wo, gffn, wgu, wd):
    s = x.shape[0]
    tm = TM_OUT
    row = lambda i: (i, 0)
    sub = lambda i: (0, i, 0)
    o_specs = [pl.BlockSpec((d, tm // d, BW), sub) for _, d in DIL_PAIRS]
    l_specs = [pl.BlockSpec((d, tm // d, HD), sub) for _, d in DIL_PAIRS]
    return pl.pallas_call(
        _merge_ffn_kernel,
        grid=(s // tm,),
        in_specs=[pl.BlockSpec((tm, D_MODEL), row),
                  pl.BlockSpec((tm, BW), row), pl.BlockSpec((tm, BW), row)]
                 + o_specs + l_specs
                 + [pl.BlockSpec((tm, N_BRANCH * D_MODEL), row),
                    _layer((N_BRANCH, BW, D_MODEL), l),
                    _layer((D_MODEL, D_MODEL), l),
                    _layer((1, D_MODEL), l),
                    _layer((D_MODEL, 2 * D_FF), l),
                    _layer((D_FF, D_MODEL), l)],
        out_specs=pl.BlockSpec((tm, D_MODEL), row),
        out_shape=jax.ShapeDtypeStruct((s, D_MODEL), F32),
        scratch_shapes=[pltpu.VMEM((N_DIL * HEADS, tm, HD), F32), pltpu.VMEM((N_DIL, tm, HD), F32),
                        pltpu.VMEM((tm, BW), BF16)],
        compiler_params=pltpu.CompilerParams(
            dimension_semantics=("arbitrary",), vmem_limit_bytes=VMEM_LIMIT_V7X),
        name="merge_ffn",
    )(x, ya, yb, *outs, *lses, gates, wb, wo, gffn, wgu, wd)


def _retention_tables():
    c = CHUNK
    log_g = jnp.log1p(-(2.0 ** (-5.0 - jnp.arange(HEADS, dtype=F32))))
    n = jnp.arange(c, dtype=F32)
    diff = n[:, None] - n[None, :]
    scale = HD ** -0.5
    decay = jnp.where(diff >= 0, jnp.exp(log_g[:, None, None] * jnp.maximum(diff, 0.0)), 0.0) * scale
    zeta = jnp.exp(log_g[:, None] * (c - 1.0 - n)[None, :]) * scale
    xi = jnp.exp(log_g[:, None] * (n + 1.0)[None, :])
    g_chunk = jnp.exp(log_g * c)
    full = lambda t: jnp.broadcast_to(t, (HEADS, c, c)).astype(F32)
    return decay, full(zeta[:, :, None]), full(xi[:, :, None]), full(g_chunk[:, None, None])


def _rotary_tables(s):
    half = HD // 2
    inv = 1.0 / (ROPE_BASE ** (jnp.arange(half, dtype=F32) / half))
    ang = jnp.arange(s, dtype=jnp.int32).astype(F32)[:, None] * inv[None, :]
    cos, sin = jnp.cos(ang), jnp.sin(ang)
    return jnp.concatenate([cos, cos], axis=-1), jnp.concatenate([-sin, sin], axis=-1)


def kernel(x, g_mix, w_in, ret_norm_g, sg_ln_g, sg_ln_b, sg_w, sg_b, dil_q_norm_g, dil_k_norm_g,
           w_branch, w_o, g_ffn, w_gate_up, w_down):
    batch, s, d_model = x.shape
    depth = w_in.shape[0]
    assert batch == 1 and d_model == D_MODEL and w_in.shape[-1] == N_IN
    assert s % (DIL_PAIRS[-1][1] * CHUNK) == 0 and s % TB_MIX == 0
    for win, d in DIL_PAIRS:
        assert win // d == CHUNK and (s // d) % TQ_ATT == 0 and TM_IN % (16 * d) == 0

    w_in_b = jnp.concatenate([w_in[:, :, t * TILE:(t + 1) * TILE] for t in _ORIG_TILE_ORDER],
                             axis=-1).astype(BF16)
    w_branch_b = w_branch.astype(BF16)
    w_o_b = w_o.astype(BF16)
    w_gu_b = w_gate_up.astype(BF16)
    w_d_b = w_down.astype(BF16)
    cos, sin = _rotary_tables(s)
    decay, zeta, xi, gch = _retention_tables()
    bias_s = jnp.broadcast_to(sg_b[:, :, :, None], sg_b.shape + (CHUNK,)).astype(F32)
    vec = lambda p: p[:, None, :]

    xs = x.reshape(s, d_model)
    for l in range(depth):
        za, a0, a1, a2, gates = _inproj(
            l, xs, vec(g_mix), w_in_b, cos, sin, vec(sg_ln_g), vec(sg_ln_b),
            vec(dil_q_norm_g), vec(dil_k_norm_g))
        ya, yb = _mixers(l, za, decay, zeta, xi, gch, vec(ret_norm_g), sg_w, bias_s)
        outs, lses = [], []
        for a, (_, d) in zip((a0, a1, a2), DIL_PAIRS):
            o, lse = _attention(a, d)
            outs.append(o)
            lses.append(lse)
        xs = _merge_ffn(l, xs, ya, yb, outs, lses, gates, w_branch_b, w_o_b, vec(g_ffn),
                        w_gu_b, w_d_b)
    return xs.reshape(batch, s, d_model)
```

```python
import functools

import jax
import jax.numpy as jnp
from jax import lax
from jax.experimental import pallas as pl
from jax.experimental.pallas import tpu as pltpu

F32 = jnp.float32
BF16 = jnp.bfloat16

D_MODEL = 1024
EPS = 1e-6
NEG_INF = -1e30
HEADS = 4
HD = 128
CHUNK = 128
ROPE_BASE = 10000.0
DIL_PAIRS = ((128, 1), (512, 4), (2048, 16))
N_DIL = len(DIL_PAIRS)
BW = HEADS * HD
N_BRANCH = 3
D_FF = 2816
N_IN = 10752
TILE = 512
LSE_LANES = HD // HEADS

VMEM_LIMIT_V7X = 56 * 1024 * 1024

_ORIG_TILE_ORDER = (0, 1, 2, 3, 4, 5, 6, 9, 12, 7, 10, 13, 8, 11, 14, 15, 16, 17, 18, 19, 20)

TM_IN = 512
SUB_IN = 256
TB_MIX = 512
TQ_ATT = 1024
TM_OUT = 512


def _resident(shape):
    nd = len(shape)
    return pl.BlockSpec(shape, lambda *_: (0,) * nd, pipeline_mode=pl.Buffered(1))


def _layer(shape, l):
    nd = len(shape)
    return pl.BlockSpec((None,) + tuple(shape), lambda *_: (l,) + (0,) * nd,
                        pipeline_mode=pl.Buffered(1))


def _inproj_kernel(x_ref, gmix_ref, w_ref, cos_ref, sin_ref, lng_ref, lnb_ref, qg_ref, kg_ref,
                   za_ref, a0_ref, a1_ref, a2_ref, gt_ref, h_ref, scr_ref):
    for sb in range(x_ref.shape[0] // SUB_IN):
        _inproj_rows(sb, x_ref, gmix_ref, w_ref, cos_ref, sin_ref, lng_ref, lnb_ref, qg_ref, kg_ref,
                     za_ref, (a0_ref, a1_ref, a2_ref), gt_ref, h_ref, scr_ref)


def _inproj_rows(sb, x_ref, gmix_ref, w_ref, cos_ref, sin_ref, lng_ref, lnb_ref, qg_ref, kg_ref,
                 za_ref, a_refs, gt_ref, h_ref, scr_ref):
    rows = slice(sb * SUB_IN, (sb + 1) * SUB_IN)
    x = x_ref[rows, :]
    ms = jnp.mean(x * x, axis=-1, keepdims=True)
    h_ref[rows, :] = (x * lax.rsqrt(ms + EPS) * gmix_ref[...]).astype(BF16)

    def proj(t):
        return jnp.dot(h_ref[rows, :], w_ref[:, t * TILE:(t + 1) * TILE], preferred_element_type=F32)

    def head_norm(acc, gain):
        outs = []
        for hh in range(HEADS):
            a = acc[:, hh * HD:(hh + 1) * HD]
            outs.append(a * lax.rsqrt(jnp.mean(a * a, axis=-1, keepdims=True) + EPS) * gain)
        return outs

    cos = cos_ref[rows, :]
    sin = sin_ref[rows, :]
    for t in range(2):
        acc = proj(t)
        for hh in range(HEADS):
            a = acc[:, hh * HD:(hh + 1) * HD]
            za_ref[rows, t * TILE + hh * HD:t * TILE + (hh + 1) * HD] = (
                a * cos + pltpu.roll(a, HD // 2, 1) * sin).astype(BF16)
    za_ref[rows, 2 * TILE:3 * TILE] = proj(2).astype(BF16)
    za_ref[rows, 3 * TILE:4 * TILE] = jax.nn.silu(proj(3)).astype(BF16)
    za_ref[rows, 4 * TILE:5 * TILE] = jax.nn.gelu(proj(4)).astype(BF16)
    v = jax.nn.gelu(proj(5))
    mu = jnp.mean(v, axis=-1, keepdims=True)
    vc = v - mu
    var = jnp.mean(vc * vc, axis=-1, keepdims=True)
    za_ref[rows, 5 * TILE:6 * TILE] = (vc * lax.rsqrt(var + EPS) * lng_ref[...] + lnb_ref[...]).astype(BF16)

    qgain = qg_ref[...] * (HD ** -0.5)
    kgain = kg_ref[...]
    for g, (a_ref, (_, d)) in enumerate(zip(a_refs, DIL_PAIRS)):
        sub_rows = slice(sb * SUB_IN // d, (sb + 1) * SUB_IN // d)
        for c in range(3):
            acc = proj(6 + 3 * g + c)
            if c == 0:
                parts = head_norm(acc, qgain)
            elif c == 1:
                parts = head_norm(acc, kgain)
            else:
                parts = [acc[:, hh * HD:(hh + 1) * HD] for hh in range(HEADS)]
            for hh in range(HEADS):
                lo = c * TILE + hh * HD
                if d == 1:
                    a_ref[0, sub_rows, lo:lo + HD] = parts[hh].astype(BF16)
                else:
                    slot = ((sb * 2 + g - 1) * 3 + c) * HEADS + hh
                    scr_ref[slot] = parts[hh]
                    for r in range(d):
                        a_ref[r, sub_rows, lo:lo + HD] = scr_ref[
                            slot, pl.ds(r, SUB_IN // d, stride=d), :].astype(BF16)

    for t in range(6):
        gt_ref[rows, t * TILE:(t + 1) * TILE] = jax.nn.sigmoid(proj(15 + t)).astype(BF16)


def _inproj(l, x, gmix, w, cos, sin, lng, lnb, qg, kg):
    s = x.shape[0]
    tm = TM_IN
    row = lambda i: (i, 0)
    sub = lambda i: (0, i, 0)
    out_shape = [jax.ShapeDtypeStruct((s, 6 * TILE), BF16)]
    out_specs = [pl.BlockSpec((tm, 6 * TILE), row)]
    for _, d in DIL_PAIRS:
        out_shape.append(jax.ShapeDtypeStruct((d, s // d, 3 * TILE), BF16))
        out_specs.append(pl.BlockSpec((d, tm // d, 3 * TILE), sub))
    out_shape.append(jax.ShapeDtypeStruct((s, 6 * TILE), BF16))
    out_specs.append(pl.BlockSpec((tm, 6 * TILE), row))
    return pl.pallas_call(
        _inproj_kernel,
        grid=(s // tm,),
        in_specs=[
            pl.BlockSpec((tm, D_MODEL), row),
            _layer((1, D_MODEL), l),
            _layer((D_MODEL, N_IN), l),
            pl.BlockSpec((tm, HD), row),
            pl.BlockSpec((tm, HD), row),
            _layer((1, BW), l),
            _layer((1, BW), l),
            _layer((1, HD), l),
            _layer((1, HD), l),
        ],
        out_specs=out_specs,
        out_shape=out_shape,
        scratch_shapes=[pltpu.VMEM((tm, D_MODEL), BF16),
                        pltpu.VMEM((tm // SUB_IN * 6 * HEADS, SUB_IN, HD), F32)],
        compiler_params=pltpu.CompilerParams(
            dimension_semantics=("arbitrary",), vmem_limit_bytes=VMEM_LIMIT_V7X),
        name="inproj",
    )(x, gmix, w, cos, sin, lng, lnb, qg, kg)


def _mixers_kernel(q_ref, k_ref, v_ref, sg_ref, u_ref, vln_ref, decay_ref, zeta_ref, xi_ref,
                   gch_ref, rg_ref, ws_ref, bs_ref, ya_ref, yb_ref, state_ref):
    n_chunks = q_ref.shape[0] // CHUNK

    @pl.when(pl.program_id(0) == 0)
    def _():
        state_ref[...] = jnp.zeros_like(state_ref)

    row = lax.broadcasted_iota(jnp.int32, (CHUNK, CHUNK), 0)
    col = lax.broadcasted_iota(jnp.int32, (CHUNK, CHUNK), 1)

    for hh in range(HEADS):
        cols = slice(hh * HD, (hh + 1) * HD)
        decay = decay_ref[hh]
        zeta = zeta_ref[hh]
        xi = xi_ref[hh]
        gch = gch_ref[hh]
        gain = rg_ref[:, cols]
        inners, kvs = [], []
        for c in range(n_chunks):
            rows = slice(c * CHUNK, (c + 1) * CHUNK)
            q = q_ref[rows, cols]
            k = k_ref[rows, cols]
            v = v_ref[rows, cols]
            scores = lax.dot_general(q, k, (((1,), (1,)), ((), ())), preferred_element_type=F32) * decay
            inners.append(jnp.dot(scores.astype(BF16), v, preferred_element_type=F32))
            vz = (v.astype(F32) * zeta).astype(BF16)
            kvs.append(lax.dot_general(k, vz, (((0,), (0,)), ((), ())), preferred_element_type=F32))
        state = state_ref[hh]
        for c in range(n_chunks):
            rows = slice(c * CHUNK, (c + 1) * CHUNK)
            cross = jnp.dot(q_ref[rows, cols], state.astype(BF16), preferred_element_type=F32) * xi
            y = inners[c] + cross
            y = y * lax.rsqrt(jnp.mean(y * y, axis=-1, keepdims=True) + EPS) * gain
            ya_ref[rows, cols] = (sg_ref[rows, cols].astype(F32) * y).astype(BF16)
            state = state * gch + kvs[c]
        state_ref[hh] = state

    for g in range(HEADS):
        cols = slice(g * HD, (g + 1) * HD)
        w = jnp.where(row >= col, ws_ref[g], 0.0).astype(BF16)
        bias = bs_ref[g]
        for c in range(n_chunks):
            rows = slice(c * CHUNK, (c + 1) * CHUNK)
            mixed = jnp.dot(w, vln_ref[rows, cols], preferred_element_type=F32) + bias
            yb_ref[rows, cols] = (u_ref[rows, cols].astype(F32) * mixed).astype(BF16)


def _mixers(l, za, decay, zeta, xi, gch, rg, ws, bs):
    s = za.shape[0]
    tb = TB_MIX
    col_block = lambda j: pl.BlockSpec((tb, TILE), lambda i, j=j: (i, j))
    table = _resident((HEADS, CHUNK, CHUNK))
    layer_table = _layer((HEADS, CHUNK, CHUNK), l)
    return pl.pallas_call(
        _mixers_kernel,
        grid=(s // tb,),
        in_specs=[col_block(j) for j in range(6)] + [table, table, table, table,
                                                     _layer((1, BW), l), layer_table, layer_table],
        out_specs=[pl.BlockSpec((tb, BW), lambda i: (i, 0))] * 2,
        out_shape=[jax.ShapeDtypeStruct((s, BW), BF16)] * 2,
        scratch_shapes=[pltpu.VMEM((HEADS, HD, HD), F32)],
        compiler_params=pltpu.CompilerParams(
            dimension_semantics=("arbitrary",), vmem_limit_bytes=VMEM_LIMIT_V7X),
        name="mixers",
    )(za, za, za, za, za, za, decay, zeta, xi, gch, rg, ws, bs)


def _attn_kernel(q_ref, kp_ref, kc_ref, vp_ref, vc_ref, o_ref, lse_ref):
    tq = q_ref.shape[0]
    first = pl.program_id(1) == 0
    row = lax.broadcasted_iota(jnp.int32, (CHUNK, 2 * CHUNK), 0)
    col = lax.broadcasted_iota(jnp.int32, (CHUNK, 2 * CHUNK), 1)
    band = (col >= row) & (col <= row + CHUNK)
    band_first = band & (col >= jnp.where(first, CHUNK, 0))
    lane = lax.broadcasted_iota(jnp.int32, (CHUNK, HD), 1)
    nt = (((1,), (1,)), ((), ()))

    for i in range(tq // CHUNK):
        rows = slice(i * CHUNK, (i + 1) * CHUNK)
        lse_tile = jnp.zeros((CHUNK, HD), F32)
        for hh in range(HEADS):
            cols = slice(hh * HD, (hh + 1) * HD)
            if i == 0:
                k = jnp.concatenate([kp_ref[:, cols], kc_ref[rows, cols]], axis=0)
                v = jnp.concatenate([vp_ref[:, cols], vc_ref[rows, cols]], axis=0)
            else:
                keys = slice((i - 1) * CHUNK, (i + 1) * CHUNK)
                k = kc_ref[keys, cols]
                v = vc_ref[keys, cols]
            s = lax.dot_general(q_ref[rows, cols], k, nt, preferred_element_type=F32)
            s = jnp.where(band_first if i == 0 else band, s, NEG_INF)
            m = jnp.max(s, axis=-1, keepdims=True)
            e = jnp.exp(s - m)
            den = jnp.sum(e, axis=-1, keepdims=True)
            o = jnp.dot(e.astype(BF16), v, preferred_element_type=F32) / den
            o_ref[rows, cols] = o.astype(BF16)
            lse_tile = jnp.where(lane // LSE_LANES == hh, m + jnp.log(den), lse_tile)
        lse_ref[rows, :] = lse_tile


def _attention(a, d):
    length = a.shape[1]
    tq = TQ_ATT
    assert length % tq == 0
    per = tq // CHUNK
    cur = lambda c: pl.BlockSpec((None, tq, BW), lambda r, b, c=c: (r, b, c))
    prev = lambda c: pl.BlockSpec((None, CHUNK, BW),
                                  lambda r, b, c=c: (r, jnp.maximum(b * per - 1, 0), c))
    return pl.pallas_call(
        _attn_kernel,
        grid=(d, length // tq),
        in_specs=[cur(0), prev(1), cur(1), prev(2), cur(2)],
        out_specs=[pl.BlockSpec((None, tq, BW), lambda r, b: (r, b, 0)),
                   pl.BlockSpec((None, tq, HD), lambda r, b: (r, b, 0))],
        out_shape=[jax.ShapeDtypeStruct((d, length, BW), BF16),
                   jax.ShapeDtypeStruct((d, length, HD), F32)],
        compiler_params=pltpu.CompilerParams(
            dimension_semantics=("arbitrary", "arbitrary"), vmem_limit_bytes=VMEM_LIMIT_V7X),
        name=f"attn_d{d}",
    )(a, a, a, a, a)


def _merge_ffn_kernel(x_ref, ya_ref, yb_ref, o0_ref, o1_ref, o2_ref, l0_ref, l1_ref, l2_ref, gt_ref,
                      wb_ref, wo_ref, gffn_ref, wgu_ref, wd_ref, out_ref, oscr_ref, lscr_ref, yc_ref):
    tm = x_ref.shape[0]

    def natural_order(src_ref, cols, scr_ref, slot, d):
        if d == 1:
            return src_ref[0, :, cols].astype(F32)
        for r in range(d):
            scr_ref[slot, pl.ds(r, tm // d, stride=d), :] = src_ref[r, :, cols].astype(F32)
        return scr_ref[slot]

    outs, lses = [], []
    for g, (o_ref, l_ref, (_, d)) in enumerate(zip((o0_ref, o1_ref, o2_ref), (l0_ref, l1_ref, l2_ref),
                                                   DIL_PAIRS)):
        outs.append([natural_order(o_ref, slice(hh * HD, (hh + 1) * HD), oscr_ref, g * HEADS + hh, d)
                     for hh in range(HEADS)])
        lses.append(natural_order(l_ref, slice(0, HD), lscr_ref, g, d))
    m = jnp.maximum(jnp.maximum(lses[0], lses[1]), lses[2])
    es = [jnp.exp(l - m) for l in lses]
    inv = 1.0 / (es[0] + es[1] + es[2])
    for hh in range(HEADS):
        cols = slice(hh * HD, (hh + 1) * HD)
        acc = jnp.zeros((tm, HD), F32)
        for g in range(N_DIL):
            wgt = (es[g] * inv)[:, hh * LSE_LANES:hh * LSE_LANES + 1]
            acc = acc + wgt * outs[g][hh]
        yc_ref[:, cols] = acc.astype(BF16)

    merged = jnp.zeros((tm, D_MODEL), F32)
    for b, y in enumerate((ya_ref[...], yb_ref[...], yc_ref[...])):
        gate = gt_ref[:, b * D_MODEL:(b + 1) * D_MODEL].astype(F32)
        merged = merged + gate * jnp.dot(y, wb_ref[b], preferred_element_type=F32)
    x1 = x_ref[...] + jnp.dot(merged.astype(BF16), wo_ref[...], preferred_element_type=F32)

    ms = jnp.mean(x1 * x1, axis=-1, keepdims=True)
    h = (x1 * lax.rsqrt(ms + EPS) * gffn_ref[...]).astype(BF16)
    gate = jnp.dot(h, wgu_ref[:, :D_FF], preferred_element_type=F32)
    up = jnp.dot(h, wgu_ref[:, D_FF:], preferred_element_type=F32)
    act = (jax.nn.silu(gate) * up).astype(BF16)
    out_ref[...] = x1 + jnp.dot(act, wd_ref[...], preferred_element_type=F32)


def _merge_ffn(l, x, ya, yb, outs, lses, gates, wb, wo, gffn, wgu, wd):
    s = x.shape[0]
    tm = TM_OUT
    row = lambda i: (i, 0)
    sub = lambda i: (0, i, 0)
    o_specs = [pl.BlockSpec((d, tm // d, BW), sub) for _, d in DIL_PAIRS]
    l_specs = [pl.BlockSpec((d, tm // d, HD), sub) for _, d in DIL_PAIRS]
    return pl.pallas_call(
        _merge_ffn_kernel,
        grid=(s // tm,),
        in_specs=[pl.BlockSpec((tm, D_MODEL), row),
                  pl.BlockSpec((tm, BW), row), pl.BlockSpec((tm, BW), row)]
                 + o_specs + l_specs
                 + [pl.BlockSpec((tm, N_BRANCH * D_MODEL), row),
                    _layer((N_BRANCH, BW, D_MODEL), l),
                    _layer((D_MODEL, D_MODEL), l),
                    _layer((1, D_MODEL), l),
                    _layer((D_MODEL, 2 * D_FF), l),
                    _layer((D_FF, D_MODEL), l)],
        out_specs=pl.BlockSpec((tm, D_MODEL), row),
        out_shape=jax.ShapeDtypeStruct((s, D_MODEL), F32),
        scratch_shapes=[pltpu.VMEM((N_DIL * HEADS, tm, HD), F32), pltpu.VMEM((N_DIL, tm, HD), F32),
                        pltpu.VMEM((tm, BW), BF16)],
        compiler_params=pltpu.CompilerParams(
            dimension_semantics=("arbitrary",), vmem_limit_bytes=VMEM_LIMIT_V7X),
        name="merge_ffn",
    )(x, ya, yb, *outs, *lses, gates, wb, wo, gffn, wgu, wd)


def _retention_tables():
    c = CHUNK
    log_g = jnp.log1p(-(2.0 ** (-5.0 - jnp.arange(HEADS, dtype=F32))))
    n = jnp.arange(c, dtype=F32)
    diff = n[:, None] - n[None, :]
    scale = HD ** -0.5
    decay = jnp.where(diff >= 0, jnp.exp(log_g[:, None, None] * jnp.maximum(diff, 0.0)), 0.0) * scale
    zeta = jnp.exp(log_g[:, None] * (c - 1.0 - n)[None, :]) * scale
    xi = jnp.exp(log_g[:, None] * (n + 1.0)[None, :])
    g_chunk = jnp.exp(log_g * c)
    full = lambda t: jnp.broadcast_to(t, (HEADS, c, c)).astype(F32)
    return decay, full(zeta[:, :, None]), full(xi[:, :, None]), full(g_chunk[:, None, None])


def _rotary_tables(s):
    half = HD // 2
    inv = 1.0 / (ROPE_BASE ** (jnp.arange(half, dtype=F32) / half))
    ang = jnp.arange(s, dtype=jnp.int32).astype(F32)[:, None] * inv[None, :]
    cos, sin = jnp.cos(ang), jnp.sin(ang)
    return jnp.concatenate([cos, cos], axis=-1), jnp.concatenate([-sin, sin], axis=-1)


def kernel(x, g_mix, w_in, ret_norm_g, sg_ln_g, sg_ln_b, sg_w, sg_b, dil_q_norm_g, dil_k_norm_g,
           w_branch, w_o, g_ffn, w_gate_up, w_down):
    batch, s, d_model = x.shape
    depth = w_in.shape[0]
    assert batch == 1 and d_model == D_MODEL and w_in.shape[-1] == N_IN
    assert s % (DIL_PAIRS[-1][1] * CHUNK) == 0 and s % TB_MIX == 0
    for win, d in DIL_PAIRS:
        assert win // d == CHUNK and (s // d) % TQ_ATT == 0
        assert SUB_IN % (16 * d) == 0 and TM_IN % SUB_IN == 0 and TM_OUT % (16 * d) == 0

    w_in_b = jnp.concatenate([w_in[:, :, t * TILE:(t + 1) * TILE] for t in _ORIG_TILE_ORDER],
                             axis=-1).astype(BF16)
    w_branch_b = w_branch.astype(BF16)
    w_o_b = w_o.astype(BF16)
    w_gu_b = w_gate_up.astype(BF16)
    w_d_b = w_down.astype(BF16)
    cos, sin = _rotary_tables(s)
    decay, zeta, xi, gch = _retention_tables()
    bias_s = jnp.broadcast_to(sg_b[:, :, :, None], sg_b.shape + (CHUNK,)).astype(F32)
    vec = lambda p: p[:, None, :]

    xs = x.reshape(s, d_model)
    for l in range(depth):
        za, a0, a1, a2, gates = _inproj(
            l, xs, vec(g_mix), w_in_b, cos, sin, vec(sg_ln_g), vec(sg_ln_b),
            vec(dil_q_norm_g), vec(dil_k_norm_g))
        ya, yb = _mixers(l, za, decay, zeta, xi, gch, vec(ret_norm_g), sg_w, bias_s)
        outs, lses = [], []
        for a, (_, d) in zip((a0, a1, a2), DIL_PAIRS):
            o, lse = _attention(a, d)
            outs.append(o)
            lses.append(lse)
        xs = _merge_ffn(l, xs, ya, yb, outs, lses, gates, w_branch_b, w_o_b, vec(g_ffn),
                        w_gu_b, w_d_b)
    return xs.reshape(batch, s, d_model)
```

```python
import jax
import jax.numpy as jnp
from jax import lax
from jax.experimental import pallas as pl
from jax.experimental.pallas import tpu as pltpu

F32 = jnp.float32
BF16 = jnp.bfloat16

D_MODEL = 1024
EPS = 1e-6
NEG_INF = -1e30
HEADS = 4
HD = 128
CHUNK = 128
ROPE_BASE = 10000.0
DIL_PAIRS = ((128, 1), (512, 4), (2048, 16))
N_DIL = len(DIL_PAIRS)
BW = HEADS * HD
N_BRANCH = 3
D_FF = 2816
N_IN = 10752
TILE = 512
LSE_LANES = HD // HEADS

T_RET_Q, T_RET_K, T_RET_V, T_RET_G, T_SG_U, T_SG_V = range(6)
T_DIL_Q, T_DIL_K, T_DIL_V = 6, 6 + N_DIL, 6 + 2 * N_DIL
T_GATE = 6 + 3 * N_DIL
N_MIX_TILES = 6

VMEM_LIMIT_V7X = 56 * 1024 * 1024

TM_IN = 512
SUB_IN = 256
TQ_ATT = 1024
TM_OUT = 512


def _resident(shape):
    nd = len(shape)
    return pl.BlockSpec(shape, lambda *_: (0,) * nd, pipeline_mode=pl.Buffered(1))


def _layer(shape, l):
    nd = len(shape)
    return pl.BlockSpec((None,) + tuple(shape), lambda *_: (l,) + (0,) * nd,
                        pipeline_mode=pl.Buffered(1))


def _inproj_kernel(x_ref, gmix_ref, w_ref, cos_ref, sin_ref, lng_ref, lnb_ref, qg_ref, kg_ref,
                   decay_ref, zeta_ref, xi_ref, gch_ref, rg_ref, ws_ref, bs_ref,
                   ya_ref, yb_ref, a0_ref, a1_ref, a2_ref, gt_ref,
                   h_ref, za_ref, scr_ref, state_ref):
    n_sub = x_ref.shape[0] // SUB_IN

    @pl.when(pl.program_id(0) == 0)
    def _():
        state_ref[...] = jnp.zeros_like(state_ref)

    for sb in range(n_sub):
        _project_mixer_inputs(sb, x_ref, gmix_ref, w_ref, cos_ref, sin_ref, lng_ref, lnb_ref,
                              h_ref, za_ref)
    _retention(za_ref, decay_ref, zeta_ref, xi_ref, gch_ref, rg_ref, ya_ref, state_ref)
    _spatial_gating(za_ref, ws_ref, bs_ref, yb_ref)
    for sb in range(n_sub):
        _project_attention_and_gates(sb, w_ref, qg_ref, kg_ref, h_ref, (a0_ref, a1_ref, a2_ref),
                                     gt_ref, scr_ref)


def _proj(h_ref, w_ref, rows, t):
    return jnp.dot(h_ref[rows, :], w_ref[:, t * TILE:(t + 1) * TILE], preferred_element_type=F32)


def _project_mixer_inputs(sb, x_ref, gmix_ref, w_ref, cos_ref, sin_ref, lng_ref, lnb_ref,
                          h_ref, za_ref):
    rows = slice(sb * SUB_IN, (sb + 1) * SUB_IN)
    x = x_ref[rows, :]
    ms = jnp.mean(x * x, axis=-1, keepdims=True)
    h_ref[rows, :] = (x * lax.rsqrt(ms + EPS) * gmix_ref[...]).astype(BF16)

    c = cos_ref[rows, :]
    s = sin_ref[rows, :]
    cos = jnp.concatenate([c, c], axis=1)
    sin = jnp.concatenate([-s, s], axis=1)
    for t in (T_RET_Q, T_RET_K):
        acc = _proj(h_ref, w_ref, rows, t)
        for hh in range(HEADS):
            a = acc[:, hh * HD:(hh + 1) * HD]
            za_ref[rows, t * TILE + hh * HD:t * TILE + (hh + 1) * HD] = (
                a * cos + pltpu.roll(a, HD // 2, 1) * sin).astype(BF16)
    tile = lambda t: slice(t * TILE, (t + 1) * TILE)
    za_ref[rows, tile(T_RET_V)] = _proj(h_ref, w_ref, rows, T_RET_V).astype(BF16)
    za_ref[rows, tile(T_RET_G)] = jax.nn.silu(_proj(h_ref, w_ref, rows, T_RET_G)).astype(BF16)
    za_ref[rows, tile(T_SG_U)] = jax.nn.gelu(_proj(h_ref, w_ref, rows, T_SG_U)).astype(BF16)
    v = jax.nn.gelu(_proj(h_ref, w_ref, rows, T_SG_V))
    mu = jnp.mean(v, axis=-1, keepdims=True)
    vc = v - mu
    var = jnp.mean(vc * vc, axis=-1, keepdims=True)
    za_ref[rows, tile(T_SG_V)] = (vc * lax.rsqrt(var + EPS) * lng_ref[...] + lnb_ref[...]).astype(BF16)


def _project_attention_and_gates(sb, w_ref, qg_ref, kg_ref, h_ref, a_refs, gt_ref, scr_ref):
    rows = slice(sb * SUB_IN, (sb + 1) * SUB_IN)

    def head_norm(acc, gain):
        outs = []
        for hh in range(HEADS):
            a = acc[:, hh * HD:(hh + 1) * HD]
            outs.append(a * lax.rsqrt(jnp.mean(a * a, axis=-1, keepdims=True) + EPS) * gain)
        return outs

    qgain = qg_ref[...] * (HD ** -0.5)
    kgain = kg_ref[...]
    for g, (a_ref, (_, d)) in enumerate(zip(a_refs, DIL_PAIRS)):
        sub_rows = slice(sb * SUB_IN // d, (sb + 1) * SUB_IN // d)
        for c, t in enumerate((T_DIL_Q + g, T_DIL_K + g, T_DIL_V + g)):
            acc = _proj(h_ref, w_ref, rows, t)
            if c == 0:
                parts = head_norm(acc, qgain)
            elif c == 1:
                parts = head_norm(acc, kgain)
            else:
                parts = [acc[:, hh * HD:(hh + 1) * HD] for hh in range(HEADS)]
            for hh in range(HEADS):
                lo = c * TILE + hh * HD
                if d == 1:
                    a_ref[0, sub_rows, lo:lo + HD] = parts[hh].astype(BF16)
                else:
                    slot = ((g - 1) * 3 + c) * HEADS + hh
                    scr_ref[slot] = parts[hh]
                    for r in range(d):
                        a_ref[r, sub_rows, lo:lo + HD] = scr_ref[
                            slot, pl.ds(r, SUB_IN // d, stride=d), :].astype(BF16)

    for t in range(6):
        gt_ref[rows, t * TILE:(t + 1) * TILE] = jax.nn.sigmoid(
            _proj(h_ref, w_ref, rows, T_GATE + t)).astype(BF16)


def _retention(za_ref, decay_ref, zeta_ref, xi_ref, gch_ref, rg_ref, ya_ref, state_ref):
    n_chunks = za_ref.shape[0] // CHUNK
    for hh in range(HEADS):
        cols = lambda t, hh=hh: slice(t * TILE + hh * HD, t * TILE + (hh + 1) * HD)
        decay = decay_ref[hh]
        zeta = zeta_ref[hh]
        xi = xi_ref[hh]
        gch = gch_ref[hh]
        gain = rg_ref[:, hh * HD:(hh + 1) * HD]
        inners, kvs = [], []
        for c in range(n_chunks):
            rows = slice(c * CHUNK, (c + 1) * CHUNK)
            q = za_ref[rows, cols(T_RET_Q)]
            k = za_ref[rows, cols(T_RET_K)]
            v = za_ref[rows, cols(T_RET_V)]
            scores = lax.dot_general(q, k, (((1,), (1,)), ((), ())), preferred_element_type=F32) * decay
            inners.append(jnp.dot(scores.astype(BF16), v, preferred_element_type=F32))
            vz = (v.astype(F32) * zeta).astype(BF16)
            kvs.append(lax.dot_general(k, vz, (((0,), (0,)), ((), ())), preferred_element_type=F32))
        state = state_ref[hh]
        for c in range(n_chunks):
            rows = slice(c * CHUNK, (c + 1) * CHUNK)
            cross = jnp.dot(za_ref[rows, cols(T_RET_Q)], state.astype(BF16),
                            preferred_element_type=F32) * xi
            y = inners[c] + cross
            y = y * lax.rsqrt(jnp.mean(y * y, axis=-1, keepdims=True) + EPS) * gain
            ya_ref[rows, hh * HD:(hh + 1) * HD] = (
                za_ref[rows, cols(T_RET_G)].astype(F32) * y).astype(BF16)
            state = state * gch + kvs[c]
        state_ref[hh] = state


def _spatial_gating(za_ref, ws_ref, bs_ref, yb_ref):
    n_chunks = za_ref.shape[0] // CHUNK
    row = lax.broadcasted_iota(jnp.int32, (CHUNK, CHUNK), 0)
    col = lax.broadcasted_iota(jnp.int32, (CHUNK, CHUNK), 1)
    for g in range(HEADS):
        cols = lambda t, g=g: slice(t * TILE + g * HD, t * TILE + (g + 1) * HD)
        w = jnp.where(row >= col, ws_ref[g], 0.0).astype(BF16)
        bias = bs_ref[g]
        for c in range(n_chunks):
            rows = slice(c * CHUNK, (c + 1) * CHUNK)
            mixed = jnp.dot(w, za_ref[rows, cols(T_SG_V)], preferred_element_type=F32) + bias
            yb_ref[rows, g * HD:(g + 1) * HD] = (
                za_ref[rows, cols(T_SG_U)].astype(F32) * mixed).astype(BF16)


def _inproj(l, x, gmix, w, cos, sin, lng, lnb, qg, kg, decay, zeta, xi, gch, rg, ws, bs):
    s = x.shape[0]
    tm = TM_IN
    row = lambda i: (i, 0)
    sub = lambda i: (0, i, 0)
    table = _resident((HEADS, CHUNK, CHUNK))
    layer_table = _layer((HEADS, CHUNK, CHUNK), l)
    out_shape = [jax.ShapeDtypeStruct((s, BW), BF16)] * 2
    out_specs = [pl.BlockSpec((tm, BW), row)] * 2
    for _, d in DIL_PAIRS:
        out_shape.append(jax.ShapeDtypeStruct((d, s // d, 3 * TILE), BF16))
        out_specs.append(pl.BlockSpec((d, tm // d, 3 * TILE), sub))
    out_shape.append(jax.ShapeDtypeStruct((s, N_BRANCH * D_MODEL), BF16))
    out_specs.append(pl.BlockSpec((tm, N_BRANCH * D_MODEL), row))
    return pl.pallas_call(
        _inproj_kernel,
        grid=(s // tm,),
        in_specs=[
            pl.BlockSpec((tm, D_MODEL), row),
            _layer((1, D_MODEL), l),
            _layer((D_MODEL, N_IN), l),
            pl.BlockSpec((tm, HD // 2), row),
            pl.BlockSpec((tm, HD // 2), row),
            _layer((1, BW), l),
            _layer((1, BW), l),
            _layer((1, HD), l),
            _layer((1, HD), l),
            table, table, table, table,
            _layer((1, BW), l), layer_table, layer_table,
        ],
        out_specs=out_specs,
        out_shape=out_shape,
        scratch_shapes=[pltpu.VMEM((tm, D_MODEL), BF16),
                        pltpu.VMEM((tm, N_MIX_TILES * TILE), BF16),
                        pltpu.VMEM((2 * 3 * HEADS, SUB_IN, HD), F32),
                        pltpu.VMEM((HEADS, HD, HD), F32)],
        compiler_params=pltpu.CompilerParams(
            dimension_semantics=("arbitrary",), vmem_limit_bytes=VMEM_LIMIT_V7X),
        name="inproj",
    )(x, gmix, w, cos, sin, lng, lnb, qg, kg, decay, zeta, xi, gch, rg, ws, bs)


def _attn_kernel(q_ref, kp_ref, kc_ref, vp_ref, vc_ref, o_ref, lse_ref):
    tq = q_ref.shape[0]
    first = pl.program_id(1) == 0
    row = lax.broadcasted_iota(jnp.int32, (CHUNK, 2 * CHUNK), 0)
    col = lax.broadcasted_iota(jnp.int32, (CHUNK, 2 * CHUNK), 1)
    band = (col >= row) & (col <= row + CHUNK)
    band_first = band & (col >= jnp.where(first, CHUNK, 0))
    lane = lax.broadcasted_iota(jnp.int32, (CHUNK, HD), 1)
    nt = (((1,), (1,)), ((), ()))

    for i in range(tq // CHUNK):
        rows = slice(i * CHUNK, (i + 1) * CHUNK)
        lse_tile = jnp.zeros((CHUNK, HD), F32)
        for hh in range(HEADS):
            cols = slice(hh * HD, (hh + 1) * HD)
            if i == 0:
                k = jnp.concatenate([kp_ref[:, cols], kc_ref[rows, cols]], axis=0)
                v = jnp.concatenate([vp_ref[:, cols], vc_ref[rows, cols]], axis=0)
            else:
                keys = slice((i - 1) * CHUNK, (i + 1) * CHUNK)
                k = kc_ref[keys, cols]
                v = vc_ref[keys, cols]
            s = lax.dot_general(q_ref[rows, cols], k, nt, preferred_element_type=F32)
            s = jnp.where(band_first if i == 0 else band, s, NEG_INF)
            m = jnp.max(s, axis=-1, keepdims=True)
            e = jnp.exp(s - m)
            den = jnp.sum(e, axis=-1, keepdims=True)
            o = jnp.dot(e.astype(BF16), v, preferred_element_type=F32) / den
            o_ref[rows, cols] = o.astype(BF16)
            lse_tile = jnp.where(lane // LSE_LANES == hh, m + jnp.log(den), lse_tile)
        lse_ref[rows, :] = lse_tile


def _attention(a, d):
    length = a.shape[1]
    tq = TQ_ATT
    assert length % tq == 0
    per = tq // CHUNK
    cur = lambda c: pl.BlockSpec((None, tq, BW), lambda r, b, c=c: (r, b, c))
    prev = lambda c: pl.BlockSpec((None, CHUNK, BW),
                                  lambda r, b, c=c: (r, jnp.maximum(b * per - 1, 0), c))
    return pl.pallas_call(
        _attn_kernel,
        grid=(d, length // tq),
        in_specs=[cur(0), prev(1), cur(1), prev(2), cur(2)],
        out_specs=[pl.BlockSpec((None, tq, BW), lambda r, b: (r, b, 0)),
                   pl.BlockSpec((None, tq, HD), lambda r, b: (r, b, 0))],
        out_shape=[jax.ShapeDtypeStruct((d, length, BW), BF16),
                   jax.ShapeDtypeStruct((d, length, HD), F32)],
        compiler_params=pltpu.CompilerParams(
            dimension_semantics=("arbitrary", "arbitrary"), vmem_limit_bytes=VMEM_LIMIT_V7X),
        name=f"attn_d{d}",
    )(a, a, a, a, a)


def _merge_ffn_kernel(x_ref, ya_ref, yb_ref, o0_ref, o1_ref, o2_ref, l0_ref, l1_ref, l2_ref, gt_ref,
                      wb_ref, wo_ref, gffn_ref, wgu_ref, wd_ref, out_ref, oscr_ref, lscr_ref, yc_ref):
    tm = x_ref.shape[0]

    def natural_order(src_ref, cols, scr_ref, slot, d):
        if d == 1:
            return src_ref[0, :, cols].astype(F32)
        for r in range(d):
            scr_ref[slot, pl.ds(r, tm // d, stride=d), :] = src_ref[r, :, cols].astype(F32)
        return scr_ref[slot]

    outs, lses = [], []
    for g, (o_ref, l_ref, (_, d)) in enumerate(zip((o0_ref, o1_ref, o2_ref), (l0_ref, l1_ref, l2_ref),
                                                   DIL_PAIRS)):
        outs.append([natural_order(o_ref, slice(hh * HD, (hh + 1) * HD), oscr_ref, g * HEADS + hh, d)
                     for hh in range(HEADS)])
        lses.append(natural_order(l_ref, slice(0, HD), lscr_ref, g, d))
    m = jnp.maximum(jnp.maximum(lses[0], lses[1]), lses[2])
    es = [jnp.exp(l - m) for l in lses]
    inv = 1.0 / (es[0] + es[1] + es[2])
    for hh in range(HEADS):
        cols = slice(hh * HD, (hh + 1) * HD)
        acc = jnp.zeros((tm, HD), F32)
        for g in range(N_DIL):
            wgt = (es[g] * inv)[:, hh * LSE_LANES:hh * LSE_LANES + 1]
            acc = acc + wgt * outs[g][hh]
        yc_ref[:, cols] = acc.astype(BF16)

    merged = jnp.zeros((tm, D_MODEL), F32)
    for b, y in enumerate((ya_ref[...], yb_ref[...], yc_ref[...])):
        gate = gt_ref[:, b * D_MODEL:(b + 1) * D_MODEL].astype(F32)
        merged = merged + gate * jnp.dot(y, wb_ref[b], preferred_element_type=F32)
    x1 = x_ref[...] + jnp.dot(merged.astype(BF16), wo_ref[...], preferred_element_type=F32)

    ms = jnp.mean(x1 * x1, axis=-1, keepdims=True)
    h = (x1 * lax.rsqrt(ms + EPS) * gffn_ref[...]).astype(BF16)
    gate = jnp.dot(h, wgu_ref[:, :D_FF], preferred_element_type=F32)
    up = jnp.dot(h, wgu_ref[:, D_FF:], preferred_element_type=F32)
    act = (jax.nn.silu(gate) * up).astype(BF16)
    out_ref[...] = x1 + jnp.dot(act, wd_ref[...], preferred_element_type=F32)


def _merge_ffn(l, x, ya, yb, outs, lses, gates, wb, wo, gffn, wgu, wd):
    s = x.shape[0]
    tm = TM_OUT
    row = lambda i: (i, 0)
    sub = lambda i: (0, i, 0)
    o_specs = [pl.BlockSpec((d, tm // d, BW), sub) for _, d in DIL_PAIRS]
    l_specs = [pl.BlockSpec((d, tm // d, HD), sub) for _, d in DIL_PAIRS]
    return pl.pallas_call(
        _merge_ffn_kernel,
        grid=(s // tm,),
        in_specs=[pl.BlockSpec((tm, D_MODEL), row),
                  pl.BlockSpec((tm, BW), row), pl.BlockSpec((tm, BW), row)]
                 + o_specs + l_specs
                 + [pl.BlockSpec((tm, N_BRANCH * D_MODEL), row),
                    _layer((N_BRANCH, BW, D_MODEL), l),
                    _layer((D_MODEL, D_MODEL), l),
                    _layer((1, D_MODEL), l),
                    _layer((D_MODEL, 2 * D_FF), l),
                    _layer((D_FF, D_MODEL), l)],
        out_specs=pl.BlockSpec((tm, D_MODEL), row),
        out_shape=jax.ShapeDtypeStruct((s, D_MODEL), F32),
        scratch_shapes=[pltpu.VMEM((N_DIL * HEADS, tm, HD), F32), pltpu.VMEM((N_DIL, tm, HD), F32),
                        pltpu.VMEM((tm, BW), BF16)],
        compiler_params=pltpu.CompilerParams(
            dimension_semantics=("arbitrary",), vmem_limit_bytes=VMEM_LIMIT_V7X),
        name="merge_ffn",
    )(x, ya, yb, *outs, *lses, gates, wb, wo, gffn, wgu, wd)


def _retention_tables():
    c = CHUNK
    log_g = jnp.log1p(-(2.0 ** (-5.0 - jnp.arange(HEADS, dtype=F32))))
    n = jnp.arange(c, dtype=F32)
    diff = n[:, None] - n[None, :]
    scale = HD ** -0.5
    decay = jnp.where(diff >= 0, jnp.exp(log_g[:, None, None] * jnp.maximum(diff, 0.0)), 0.0) * scale
    zeta = jnp.exp(log_g[:, None] * (c - 1.0 - n)[None, :]) * scale
    xi = jnp.exp(log_g[:, None] * (n + 1.0)[None, :])
    g_chunk = jnp.exp(log_g * c)
    full = lambda t: jnp.broadcast_to(t, (HEADS, c, c)).astype(F32)
    return decay, full(zeta[:, :, None]), full(xi[:, :, None]), full(g_chunk[:, None, None])


def _rotary_tables(s):
    half = HD // 2
    inv = 1.0 / (ROPE_BASE ** (jnp.arange(half, dtype=F32) / half))
    ang = jnp.arange(s, dtype=jnp.int32).astype(F32)[:, None] * inv[None, :]
    return jnp.cos(ang), jnp.sin(ang)


def kernel(x, g_mix, w_in, ret_norm_g, sg_ln_g, sg_ln_b, sg_w, sg_b, dil_q_norm_g, dil_k_norm_g,
           w_branch, w_o, g_ffn, w_gate_up, w_down):
    batch, s, d_model = x.shape
    depth = w_in.shape[0]
    assert batch == 1 and d_model == D_MODEL and w_in.shape[-1] == N_IN
    assert s % (DIL_PAIRS[-1][1] * CHUNK) == 0 and s % TM_IN == 0 and s % TM_OUT == 0
    for win, d in DIL_PAIRS:
        assert win // d == CHUNK and (s // d) % TQ_ATT == 0
        assert SUB_IN % (16 * d) == 0 and TM_IN % SUB_IN == 0 and TM_OUT % (16 * d) == 0

    w_in_b = w_in.astype(BF16)
    w_branch_b = w_branch.astype(BF16)
    w_o_b = w_o.astype(BF16)
    w_gu_b = w_gate_up.astype(BF16)
    w_d_b = w_down.astype(BF16)
    cos, sin = _rotary_tables(s)
    decay, zeta, xi, gch = _retention_tables()
    bias_s = jnp.broadcast_to(sg_b[:, :, :, None], sg_b.shape + (CHUNK,)).astype(F32)
    vec = lambda p: p[:, None, :]

    xs = x.reshape(s, d_model)
    for l in range(depth):
        ya, yb, a0, a1, a2, gates = _inproj(
            l, xs, vec(g_mix), w_in_b, cos, sin, vec(sg_ln_g), vec(sg_ln_b),
            vec(dil_q_norm_g), vec(dil_k_norm_g), decay, zeta, xi, gch, vec(ret_norm_g), sg_w, bias_s)
        outs, lses = [], []
        for a, (_, d) in zip((a0, a1, a2), DIL_PAIRS):
            o, lse = _attention(a, d)
            outs.append(o)
            lses.append(lse)
        xs = _merge_ffn(l, xs, ya, yb, outs, lses, gates, w_branch_b, w_o_b, vec(g_ffn),
                        w_gu_b, w_d_b)
    return xs.reshape(batch, s, d_model)
```

```python
import jax
import jax.numpy as jnp
from jax import lax
from jax.experimental import pallas as pl
from jax.experimental.pallas import tpu as pltpu

F32 = jnp.float32
BF16 = jnp.bfloat16

D_MODEL = 1024
EPS = 1e-6
NEG_INF = -1e30
HEADS = 4
HD = 128
CHUNK = 128
ROPE_BASE = 10000.0
DIL_PAIRS = ((128, 1), (512, 4), (2048, 16))
N_DIL = len(DIL_PAIRS)
BW = HEADS * HD
N_BRANCH = 3
D_FF = 2816
N_IN = 10752
TILE = 512
LSE_LANES = HD // HEADS
LOG2E = 1.4426950408889634
LN2 = 0.6931471805599453
DEN_TRUST_MIN = 1e-30

T_RET_Q, T_RET_K, T_RET_V, T_RET_G, T_SG_U, T_SG_V = range(6)
T_DIL_Q, T_DIL_K, T_DIL_V = 6, 6 + N_DIL, 6 + 2 * N_DIL
T_GATE = 6 + 3 * N_DIL
N_MIX_TILES = 6

VMEM_LIMIT_V7X = 56 * 1024 * 1024

TM_IN = 512
SUB_IN = 256
TQ_ATT = 1024
TM_OUT = 512


def _resident(shape):
    nd = len(shape)
    return pl.BlockSpec(shape, lambda *_: (0,) * nd, pipeline_mode=pl.Buffered(1))


def _layer(shape, l):
    nd = len(shape)
    return pl.BlockSpec((None,) + tuple(shape), lambda *_: (l,) + (0,) * nd,
                        pipeline_mode=pl.Buffered(1))


def _inproj_kernel(x_ref, gmix_ref, w_ref, cos_ref, sin_ref, lng_ref, lnb_ref, qg_ref, kg_ref,
                   decay_ref, zeta_ref, xi_ref, gch_ref, rg_ref, ws_ref, bs_ref,
                   ya_ref, yb_ref, a0_ref, a1_ref, a2_ref, gt_ref,
                   h_ref, za_ref, scr_ref, state_ref):
    n_sub = x_ref.shape[0] // SUB_IN

    @pl.when(pl.program_id(0) == 0)
    def _():
        state_ref[...] = jnp.zeros_like(state_ref)

    for sb in range(n_sub):
        _project_mixer_inputs(sb, x_ref, gmix_ref, w_ref, cos_ref, sin_ref, lng_ref, lnb_ref,
                              h_ref, za_ref)
    _retention(za_ref, decay_ref, zeta_ref, xi_ref, gch_ref, rg_ref, ya_ref, state_ref)
    _spatial_gating(za_ref, ws_ref, bs_ref, yb_ref)
    for sb in range(n_sub):
        _project_attention_and_gates(sb, w_ref, qg_ref, kg_ref, h_ref, (a0_ref, a1_ref, a2_ref),
                                     gt_ref, scr_ref)


def _proj(h_ref, w_ref, rows, t):
    return jnp.dot(h_ref[rows, :], w_ref[:, t * TILE:(t + 1) * TILE], preferred_element_type=F32)


def _project_mixer_inputs(sb, x_ref, gmix_ref, w_ref, cos_ref, sin_ref, lng_ref, lnb_ref,
                          h_ref, za_ref):
    rows = slice(sb * SUB_IN, (sb + 1) * SUB_IN)
    x = x_ref[rows, :]
    ms = jnp.mean(x * x, axis=-1, keepdims=True)
    h_ref[rows, :] = (x * lax.rsqrt(ms + EPS) * gmix_ref[...]).astype(BF16)

    c = cos_ref[rows, :]
    s = sin_ref[rows, :]
    cos = jnp.concatenate([c, c], axis=1)
    sin = jnp.concatenate([-s, s], axis=1)
    for t in (T_RET_Q, T_RET_K):
        acc = _proj(h_ref, w_ref, rows, t)
        for hh in range(HEADS):
            a = acc[:, hh * HD:(hh + 1) * HD]
            za_ref[rows, t * TILE + hh * HD:t * TILE + (hh + 1) * HD] = (
                a * cos + pltpu.roll(a, HD // 2, 1) * sin).astype(BF16)
    tile = lambda t: slice(t * TILE, (t + 1) * TILE)
    za_ref[rows, tile(T_RET_V)] = _proj(h_ref, w_ref, rows, T_RET_V).astype(BF16)
    za_ref[rows, tile(T_RET_G)] = jax.nn.silu(_proj(h_ref, w_ref, rows, T_RET_G)).astype(BF16)
    za_ref[rows, tile(T_SG_U)] = jax.nn.gelu(_proj(h_ref, w_ref, rows, T_SG_U)).astype(BF16)
    v = jax.nn.gelu(_proj(h_ref, w_ref, rows, T_SG_V))
    mu = jnp.mean(v, axis=-1, keepdims=True)
    vc = v - mu
    var = jnp.mean(vc * vc, axis=-1, keepdims=True)
    za_ref[rows, tile(T_SG_V)] = (vc * lax.rsqrt(var + EPS) * lng_ref[...] + lnb_ref[...]).astype(BF16)


def _project_attention_and_gates(sb, w_ref, qg_ref, kg_ref, h_ref, a_refs, gt_ref, scr_ref):
    rows = slice(sb * SUB_IN, (sb + 1) * SUB_IN)

    def head_norm(acc, gain):
        outs = []
        for hh in range(HEADS):
            a = acc[:, hh * HD:(hh + 1) * HD]
            outs.append(a * lax.rsqrt(jnp.mean(a * a, axis=-1, keepdims=True) + EPS) * gain)
        return outs

    qgain = qg_ref[...] * (LOG2E * HD ** -0.5)
    kgain = kg_ref[...]
    for g, (a_ref, (_, d)) in enumerate(zip(a_refs, DIL_PAIRS)):
        sub_rows = slice(sb * SUB_IN // d, (sb + 1) * SUB_IN // d)
        for c, t in enumerate((T_DIL_Q + g, T_DIL_K + g, T_DIL_V + g)):
            acc = _proj(h_ref, w_ref, rows, t)
            if c == 0:
                parts = head_norm(acc, qgain)
            elif c == 1:
                parts = head_norm(acc, kgain)
            else:
                parts = [acc[:, hh * HD:(hh + 1) * HD] for hh in range(HEADS)]
            for hh in range(HEADS):
                lo = c * TILE + hh * HD
                if d == 1:
                    a_ref[0, sub_rows, lo:lo + HD] = parts[hh].astype(BF16)
                else:
                    slot = ((g - 1) * 3 + c) * HEADS + hh
                    scr_ref[slot] = parts[hh]
                    for r in range(d):
                        a_ref[r, sub_rows, lo:lo + HD] = scr_ref[
                            slot, pl.ds(r, SUB_IN // d, stride=d), :].astype(BF16)

    for t in range(6):
        gt_ref[rows, t * TILE:(t + 1) * TILE] = jax.nn.sigmoid(
            _proj(h_ref, w_ref, rows, T_GATE + t)).astype(BF16)


def _retention(za_ref, decay_ref, zeta_ref, xi_ref, gch_ref, rg_ref, ya_ref, state_ref):
    n_chunks = za_ref.shape[0] // CHUNK
    for hh in range(HEADS):
        cols = lambda t, hh=hh: slice(t * TILE + hh * HD, t * TILE + (hh + 1) * HD)
        decay = decay_ref[hh]
        zeta = zeta_ref[hh]
        xi = xi_ref[hh]
        gch = gch_ref[hh]
        gain = rg_ref[:, hh * HD:(hh + 1) * HD]
        inners, kvs = [], []
        for c in range(n_chunks):
            rows = slice(c * CHUNK, (c + 1) * CHUNK)
            q = za_ref[rows, cols(T_RET_Q)]
            k = za_ref[rows, cols(T_RET_K)]
            v = za_ref[rows, cols(T_RET_V)]
            scores = lax.dot_general(q, k, (((1,), (1,)), ((), ())), preferred_element_type=F32) * decay
            inners.append(jnp.dot(scores.astype(BF16), v, preferred_element_type=F32))
            vz = (v.astype(F32) * zeta).astype(BF16)
            kvs.append(lax.dot_general(k, vz, (((0,), (0,)), ((), ())), preferred_element_type=F32))
        state = state_ref[hh]
        for c in range(n_chunks):
            rows = slice(c * CHUNK, (c + 1) * CHUNK)
            cross = jnp.dot(za_ref[rows, cols(T_RET_Q)], state.astype(BF16),
                            preferred_element_type=F32) * xi
            y = inners[c] + cross
            y = y * lax.rsqrt(jnp.mean(y * y, axis=-1, keepdims=True) + EPS) * gain
            ya_ref[rows, hh * HD:(hh + 1) * HD] = (
                za_ref[rows, cols(T_RET_G)].astype(F32) * y).astype(BF16)
            state = state * gch + kvs[c]
        state_ref[hh] = state


def _spatial_gating(za_ref, ws_ref, bs_ref, yb_ref):
    n_chunks = za_ref.shape[0] // CHUNK
    row = lax.broadcasted_iota(jnp.int32, (CHUNK, CHUNK), 0)
    col = lax.broadcasted_iota(jnp.int32, (CHUNK, CHUNK), 1)
    for g in range(HEADS):
        cols = lambda t, g=g: slice(t * TILE + g * HD, t * TILE + (g + 1) * HD)
        w = jnp.where(row >= col, ws_ref[g], 0.0).astype(BF16)
        bias = bs_ref[g]
        for c in range(n_chunks):
            rows = slice(c * CHUNK, (c + 1) * CHUNK)
            mixed = jnp.dot(w, za_ref[rows, cols(T_SG_V)], preferred_element_type=F32) + bias
            yb_ref[rows, g * HD:(g + 1) * HD] = (
                za_ref[rows, cols(T_SG_U)].astype(F32) * mixed).astype(BF16)


def _inproj(l, x, gmix, w, cos, sin, lng, lnb, qg, kg, decay, zeta, xi, gch, rg, ws, bs):
    s = x.shape[0]
    tm = TM_IN
    row = lambda i: (i, 0)
    sub = lambda i: (0, i, 0)
    table = _resident((HEADS, CHUNK, CHUNK))
    layer_table = _layer((HEADS, CHUNK, CHUNK), l)
    out_shape = [jax.ShapeDtypeStruct((s, BW), BF16)] * 2
    out_specs = [pl.BlockSpec((tm, BW), row)] * 2
    for _, d in DIL_PAIRS:
        out_shape.append(jax.ShapeDtypeStruct((d, s // d, 3 * TILE), BF16))
        out_specs.append(pl.BlockSpec((d, tm // d, 3 * TILE), sub))
    out_shape.append(jax.ShapeDtypeStruct((s, N_BRANCH * D_MODEL), BF16))
    out_specs.append(pl.BlockSpec((tm, N_BRANCH * D_MODEL), row))
    return pl.pallas_call(
        _inproj_kernel,
        grid=(s // tm,),
        in_specs=[
            pl.BlockSpec((tm, D_MODEL), row),
            _layer((1, D_MODEL), l),
            _layer((D_MODEL, N_IN), l),
            pl.BlockSpec((tm, HD // 2), row),
            pl.BlockSpec((tm, HD // 2), row),
            _layer((1, BW), l),
            _layer((1, BW), l),
            _layer((1, HD), l),
            _layer((1, HD), l),
            table, table, table, table,
            _layer((1, BW), l), layer_table, layer_table,
        ],
        out_specs=out_specs,
        out_shape=out_shape,
        scratch_shapes=[pltpu.VMEM((tm, D_MODEL), BF16),
                        pltpu.VMEM((tm, N_MIX_TILES * TILE), BF16),
                        pltpu.VMEM((2 * 3 * HEADS, SUB_IN, HD), F32),
                        pltpu.VMEM((HEADS, HD, HD), F32)],
        compiler_params=pltpu.CompilerParams(
            dimension_semantics=("arbitrary",), vmem_limit_bytes=VMEM_LIMIT_V7X),
        name="inproj",
    )(x, gmix, w, cos, sin, lng, lnb, qg, kg, decay, zeta, xi, gch, rg, ws, bs)


def _attn_kernel(bound_ref, q_ref, kp_ref, kc_ref, vp_ref, vc_ref, o_ref, lse_ref):
    tq = q_ref.shape[0]
    first = pl.program_id(1) == 0
    row = lax.broadcasted_iota(jnp.int32, (CHUNK, 2 * CHUNK), 0)
    col = lax.broadcasted_iota(jnp.int32, (CHUNK, 2 * CHUNK), 1)
    band = (col >= row) & (col <= row + CHUNK)
    band_first = band & (col >= jnp.where(first, CHUNK, 0))
    lane = lax.broadcasted_iota(jnp.int32, (CHUNK, HD), 1)
    nt = (((1,), (1,)), ((), ()))
    bound = bound_ref[...]

    def operands(i, hh):
        rows = slice(i * CHUNK, (i + 1) * CHUNK)
        cols = slice(hh * HD, (hh + 1) * HD)
        if i == 0:
            k = jnp.concatenate([kp_ref[:, cols], kc_ref[rows, cols]], axis=0)
            v = jnp.concatenate([vp_ref[:, cols], vc_ref[rows, cols]], axis=0)
        else:
            keys = slice((i - 1) * CHUNK, (i + 1) * CHUNK)
            k = kc_ref[keys, cols]
            v = vc_ref[keys, cols]
        return rows, cols, q_ref[rows, cols], k, v

    shift = jnp.where(band, -bound, NEG_INF)
    shift_first = jnp.where(band_first, -bound, NEG_INF)
    ones = jnp.ones((2 * CHUNK, HD), BF16)
    trusted = jnp.ones((CHUNK, HD), F32)
    for i in range(tq // CHUNK):
        den_tile = jnp.zeros((CHUNK, HD), F32)
        for hh in range(HEADS):
            rows, cols, q, k, v = operands(i, hh)
            s = lax.dot_general(q, k, nt, preferred_element_type=F32)
            p = jnp.exp2(s + (shift_first if i == 0 else shift)).astype(BF16)
            od = jnp.dot(p, jnp.concatenate([v, ones], axis=1), preferred_element_type=F32)
            den = od[:, HD:]
            o_ref[rows, cols] = (od[:, :HD] / den).astype(BF16)
            den_tile = jnp.where(lane // LSE_LANES == hh, den, den_tile)
        lse_ref[i * CHUNK:(i + 1) * CHUNK, :] = (bound[:, :HD] + jnp.log2(den_tile)) * LN2
        trusted = jnp.minimum(trusted, jnp.where(den_tile >= DEN_TRUST_MIN, 1.0, 0.0))

    @pl.when(jnp.min(trusted) < 0.5)
    def _():
        for i in range(tq // CHUNK):
            lse_tile = jnp.zeros((CHUNK, HD), F32)
            for hh in range(HEADS):
                rows, cols, q, k, v = operands(i, hh)
                s = lax.dot_general(q, k, nt, preferred_element_type=F32)
                s = jnp.where(band_first if i == 0 else band, s, NEG_INF)
                m = jnp.max(s, axis=-1, keepdims=True)
                e = jnp.exp2(s - m)
                den = jnp.sum(e, axis=-1, keepdims=True)
                o = jnp.dot(e.astype(BF16), v, preferred_element_type=F32) / den
                o_ref[rows, cols] = o.astype(BF16)
                lse_tile = jnp.where(lane // LSE_LANES == hh, (m + jnp.log2(den)) * LN2, lse_tile)
            lse_ref[i * CHUNK:(i + 1) * CHUNK, :] = lse_tile


def _attention(l, a, d, bound):
    length = a.shape[1]
    tq = TQ_ATT
    assert length % tq == 0
    per = tq // CHUNK
    cur = lambda c: pl.BlockSpec((None, tq, BW), lambda r, b, c=c: (r, b, c))
    prev = lambda c: pl.BlockSpec((None, CHUNK, BW),
                                  lambda r, b, c=c: (r, jnp.maximum(b * per - 1, 0), c))
    return pl.pallas_call(
        _attn_kernel,
        grid=(d, length // tq),
        in_specs=[_layer((1, 2 * CHUNK), l), cur(0), prev(1), cur(1), prev(2), cur(2)],
        out_specs=[pl.BlockSpec((None, tq, BW), lambda r, b: (r, b, 0)),
                   pl.BlockSpec((None, tq, HD), lambda r, b: (r, b, 0))],
        out_shape=[jax.ShapeDtypeStruct((d, length, BW), BF16),
                   jax.ShapeDtypeStruct((d, length, HD), F32)],
        compiler_params=pltpu.CompilerParams(
            dimension_semantics=("arbitrary", "arbitrary"), vmem_limit_bytes=VMEM_LIMIT_V7X),
        name=f"attn_d{d}",
    )(bound, a, a, a, a, a)


def _merge_ffn_kernel(x_ref, ya_ref, yb_ref, o0_ref, o1_ref, o2_ref, l0_ref, l1_ref, l2_ref, gt_ref,
                      wb_ref, wo_ref, gffn_ref, wgu_ref, wd_ref, out_ref, oscr_ref, lscr_ref, yc_ref):
    tm = x_ref.shape[0]

    def natural_order(src_ref, cols, scr_ref, slot, d):
        if d == 1:
            return src_ref[0, :, cols].astype(F32)
        for r in range(d):
            scr_ref[slot, pl.ds(r, tm // d, stride=d), :] = src_ref[r, :, cols].astype(F32)
        return scr_ref[slot]

    outs, lses = [], []
    for g, (o_ref, l_ref, (_, d)) in enumerate(zip((o0_ref, o1_ref, o2_ref), (l0_ref, l1_ref, l2_ref),
                                                   DIL_PAIRS)):
        outs.append([natural_order(o_ref, slice(hh * HD, (hh + 1) * HD), oscr_ref, g * HEADS + hh, d)
                     for hh in range(HEADS)])
        lses.append(natural_order(l_ref, slice(0, HD), lscr_ref, g, d))
    m = jnp.maximum(jnp.maximum(lses[0], lses[1]), lses[2])
    es = [jnp.exp(l - m) for l in lses]
    inv = 1.0 / (es[0] + es[1] + es[2])
    for hh in range(HEADS):
        cols = slice(hh * HD, (hh + 1) * HD)
        acc = jnp.zeros((tm, HD), F32)
        for g in range(N_DIL):
            wgt = (es[g] * inv)[:, hh * LSE_LANES:hh * LSE_LANES + 1]
            acc = acc + wgt * outs[g][hh]
        yc_ref[:, cols] = acc.astype(BF16)

    merged = jnp.zeros((tm, D_MODEL), F32)
    for b, y in enumerate((ya_ref[...], yb_ref[...], yc_ref[...])):
        gate = gt_ref[:, b * D_MODEL:(b + 1) * D_MODEL].astype(F32)
        merged = merged + gate * jnp.dot(y, wb_ref[b], preferred_element_type=F32)
    x1 = x_ref[...] + jnp.dot(merged.astype(BF16), wo_ref[...], preferred_element_type=F32)

    ms = jnp.mean(x1 * x1, axis=-1, keepdims=True)
    h = (x1 * lax.rsqrt(ms + EPS) * gffn_ref[...]).astype(BF16)
    gate = jnp.dot(h, wgu_ref[:, :D_FF], preferred_element_type=F32)
    up = jnp.dot(h, wgu_ref[:, D_FF:], preferred_element_type=F32)
    act = (jax.nn.silu(gate) * up).astype(BF16)
    out_ref[...] = x1 + jnp.dot(act, wd_ref[...], preferred_element_type=F32)


def _merge_ffn(l, x, ya, yb, outs, lses, gates, wb, wo, gffn, wgu, wd):
    s = x.shape[0]
    tm = TM_OUT
    row = lambda i: (i, 0)
    sub = lambda i: (0, i, 0)
    o_specs = [pl.BlockSpec((d, tm // d, BW), sub) for _, d in DIL_PAIRS]
    l_specs = [pl.BlockSpec((d, tm // d, HD), sub) for _, d in DIL_PAIRS]
    return pl.pallas_call(
        _merge_ffn_kernel,
        grid=(s // tm,),
        in_specs=[pl.BlockSpec((tm, D_MODEL), row),
                  pl.BlockSpec((tm, BW), row), pl.BlockSpec((tm, BW), row)]
                 + o_specs + l_specs
                 + [pl.BlockSpec((tm, N_BRANCH * D_MODEL), row),
                    _layer((N_BRANCH, BW, D_MODEL), l),
                    _layer((D_MODEL, D_MODEL), l),
                    _layer((1, D_MODEL), l),
                    _layer((D_MODEL, 2 * D_FF), l),
                    _layer((D_FF, D_MODEL), l)],
        out_specs=pl.BlockSpec((tm, D_MODEL), row),
        out_shape=jax.ShapeDtypeStruct((s, D_MODEL), F32),
        scratch_shapes=[pltpu.VMEM((N_DIL * HEADS, tm, HD), F32), pltpu.VMEM((N_DIL, tm, HD), F32),
                        pltpu.VMEM((tm, BW), BF16)],
        compiler_params=pltpu.CompilerParams(
            dimension_semantics=("arbitrary",), vmem_limit_bytes=VMEM_LIMIT_V7X),
        name="merge_ffn",
    )(x, ya, yb, *outs, *lses, gates, wb, wo, gffn, wgu, wd)


def _retention_tables():
    c = CHUNK
    log_g = jnp.log1p(-(2.0 ** (-5.0 - jnp.arange(HEADS, dtype=F32))))
    n = jnp.arange(c, dtype=F32)
    diff = n[:, None] - n[None, :]
    scale = HD ** -0.5
    decay = jnp.where(diff >= 0, jnp.exp(log_g[:, None, None] * jnp.maximum(diff, 0.0)), 0.0) * scale
    zeta = jnp.exp(log_g[:, None] * (c - 1.0 - n)[None, :]) * scale
    xi = jnp.exp(log_g[:, None] * (n + 1.0)[None, :])
    g_chunk = jnp.exp(log_g * c)
    full = lambda t: jnp.broadcast_to(t, (HEADS, c, c)).astype(F32)
    return decay, full(zeta[:, :, None]), full(xi[:, :, None]), full(g_chunk[:, None, None])


def _rotary_tables(s):
    half = HD // 2
    inv = 1.0 / (ROPE_BASE ** (jnp.arange(half, dtype=F32) / half))
    ang = jnp.arange(s, dtype=jnp.int32).astype(F32)[:, None] * inv[None, :]
    return jnp.cos(ang), jnp.sin(ang)


def kernel(x, g_mix, w_in, ret_norm_g, sg_ln_g, sg_ln_b, sg_w, sg_b, dil_q_norm_g, dil_k_norm_g,
           w_branch, w_o, g_ffn, w_gate_up, w_down):
    batch, s, d_model = x.shape
    depth = w_in.shape[0]
    assert batch == 1 and d_model == D_MODEL and w_in.shape[-1] == N_IN
    assert s % (DIL_PAIRS[-1][1] * CHUNK) == 0 and s % TM_IN == 0 and s % TM_OUT == 0
    for win, d in DIL_PAIRS:
        assert win // d == CHUNK and (s // d) % TQ_ATT == 0
        assert SUB_IN % (16 * d) == 0 and TM_IN % SUB_IN == 0 and TM_OUT % (16 * d) == 0

    w_in_b = w_in.astype(BF16)
    w_branch_b = w_branch.astype(BF16)
    w_o_b = w_o.astype(BF16)
    w_gu_b = w_gate_up.astype(BF16)
    w_d_b = w_down.astype(BF16)
    cos, sin = _rotary_tables(s)
    decay, zeta, xi, gch = _retention_tables()
    bias_s = jnp.broadcast_to(sg_b[:, :, :, None], sg_b.shape + (CHUNK,)).astype(F32)
    vec = lambda p: p[:, None, :]
    score_bound = (LOG2E * HD ** 0.5) * (jnp.max(jnp.abs(dil_q_norm_g), axis=-1)
                                         * jnp.max(jnp.abs(dil_k_norm_g), axis=-1))
    score_bound = jnp.broadcast_to(score_bound[:, None, None], (depth, 1, 2 * CHUNK)).astype(F32)

    xs = x.reshape(s, d_model)
    for l in range(depth):
        ya, yb, a0, a1, a2, gates = _inproj(
            l, xs, vec(g_mix), w_in_b, cos, sin, vec(sg_ln_g), vec(sg_ln_b),
            vec(dil_q_norm_g), vec(dil_k_norm_g), decay, zeta, xi, gch, vec(ret_norm_g), sg_w, bias_s)
        outs, lses = [], []
        for a, (_, d) in zip((a0, a1, a2), DIL_PAIRS):
            o, lse = _attention(l, a, d, score_bound)
            outs.append(o)
            lses.append(lse)
        xs = _merge_ffn(l, xs, ya, yb, outs, lses, gates, w_branch_b, w_o_b, vec(g_ffn),
                        w_gu_b, w_d_b)
    return xs.reshape(batch, s, d_model)
```

```python
import jax
import jax.numpy as jnp
from jax import lax
from jax.experimental import pallas as pl
from jax.experimental.pallas import tpu as pltpu

F32 = jnp.float32
BF16 = jnp.bfloat16

D_MODEL = 1024
EPS = 1e-6
NEG_INF = -1e30
HEADS = 4
HD = 128
CHUNK = 128
ROPE_BASE = 10000.0
DIL_PAIRS = ((128, 1), (512, 4), (2048, 16))
N_DIL = len(DIL_PAIRS)
BW = HEADS * HD
N_BRANCH = 3
D_FF = 2816
N_IN = 10752
TILE = 512
LSE_LANES = HD // HEADS
LOG2E = 1.4426950408889634
LN2 = 0.6931471805599453
DEN_TRUST_MIN = 1e-30

T_RET_Q, T_RET_K, T_RET_V, T_RET_G, T_SG_U, T_SG_V = range(6)
T_DIL_Q, T_DIL_K, T_DIL_V = 6, 6 + N_DIL, 6 + 2 * N_DIL
T_GATE = 6 + 3 * N_DIL
N_MIX_TILES = 6

VMEM_LIMIT_V7X = 56 * 1024 * 1024

TM_IN = 512
SUB_IN = 256
TQ_ATT = 1024
TM_OUT = 512


def _resident(shape):
    nd = len(shape)
    return pl.BlockSpec(shape, lambda *_: (0,) * nd, pipeline_mode=pl.Buffered(1))


def _layer(shape, l):
    nd = len(shape)
    return pl.BlockSpec((None,) + tuple(shape), lambda *_: (l,) + (0,) * nd,
                        pipeline_mode=pl.Buffered(1))


def _inproj_kernel(x_ref, gmix_ref, w_ref, cos_ref, sin_ref, lng_ref, lnb_ref, qg_ref, kg_ref,
                   decay_ref, zeta_ref, xi_ref, gch_ref, rg_ref, ws_ref, bs_ref,
                   ya_ref, yb_ref, a0_ref, a1_ref, a2_ref, gt_ref,
                   h_ref, za_ref, scr_ref, state_ref):
    n_sub = x_ref.shape[0] // SUB_IN

    @pl.when(pl.program_id(0) == 0)
    def _():
        state_ref[...] = jnp.zeros_like(state_ref)

    for sb in range(n_sub):
        _project_mixer_inputs(sb, x_ref, gmix_ref, w_ref, cos_ref, sin_ref, lng_ref, lnb_ref,
                              h_ref, za_ref)
    _retention(za_ref, decay_ref, zeta_ref, xi_ref, gch_ref, rg_ref, ya_ref, state_ref)
    _spatial_gating(za_ref, ws_ref, bs_ref, yb_ref)
    for sb in range(n_sub):
        _project_attention_and_gates(sb, w_ref, qg_ref, kg_ref, h_ref, (a0_ref, a1_ref, a2_ref),
                                     gt_ref, scr_ref)


def _proj(h_ref, w_ref, rows, t):
    return jnp.dot(h_ref[rows, :], w_ref[:, t * TILE:(t + 1) * TILE], preferred_element_type=F32)


def _project_mixer_inputs(sb, x_ref, gmix_ref, w_ref, cos_ref, sin_ref, lng_ref, lnb_ref,
                          h_ref, za_ref):
    rows = slice(sb * SUB_IN, (sb + 1) * SUB_IN)
    x = x_ref[rows, :]
    ms = jnp.mean(x * x, axis=-1, keepdims=True)
    h_ref[rows, :] = (x * lax.rsqrt(ms + EPS) * gmix_ref[...]).astype(BF16)

    c = cos_ref[rows, :]
    s = sin_ref[rows, :]
    cos = jnp.concatenate([c, c], axis=1)
    sin = jnp.concatenate([-s, s], axis=1)
    for t in (T_RET_Q, T_RET_K):
        acc = _proj(h_ref, w_ref, rows, t)
        for hh in range(HEADS):
            a = acc[:, hh * HD:(hh + 1) * HD]
            za_ref[rows, t * TILE + hh * HD:t * TILE + (hh + 1) * HD] = (
                a * cos + pltpu.roll(a, HD // 2, 1) * sin).astype(BF16)
    tile = lambda t: slice(t * TILE, (t + 1) * TILE)
    za_ref[rows, tile(T_RET_V)] = _proj(h_ref, w_ref, rows, T_RET_V).astype(BF16)
    za_ref[rows, tile(T_RET_G)] = jax.nn.silu(_proj(h_ref, w_ref, rows, T_RET_G)).astype(BF16)
    za_ref[rows, tile(T_SG_U)] = jax.nn.gelu(_proj(h_ref, w_ref, rows, T_SG_U)).astype(BF16)
    v = jax.nn.gelu(_proj(h_ref, w_ref, rows, T_SG_V))
    mu = jnp.mean(v, axis=-1, keepdims=True)
    vc = v - mu
    var = jnp.mean(vc * vc, axis=-1, keepdims=True)
    za_ref[rows, tile(T_SG_V)] = (vc * lax.rsqrt(var + EPS) * lng_ref[...] + lnb_ref[...]).astype(BF16)


def _project_attention_and_gates(sb, w_ref, qg_ref, kg_ref, h_ref, a_refs, gt_ref, scr_ref):
    rows = slice(sb * SUB_IN, (sb + 1) * SUB_IN)

    def head_norm(acc, gain):
        outs = []
        for hh in range(HEADS):
            a = acc[:, hh * HD:(hh + 1) * HD]
            outs.append(a * lax.rsqrt(jnp.mean(a * a, axis=-1, keepdims=True) + EPS) * gain)
        return outs

    qgain = qg_ref[...] * (LOG2E * HD ** -0.5)
    kgain = kg_ref[...]
    for g, (a_ref, (_, d)) in enumerate(zip(a_refs, DIL_PAIRS)):
        sub_rows = slice(sb * SUB_IN // d, (sb + 1) * SUB_IN // d)
        for c, t in enumerate((T_DIL_Q + g, T_DIL_K + g, T_DIL_V + g)):
            acc = _proj(h_ref, w_ref, rows, t)
            if c == 0:
                parts = head_norm(acc, qgain)
            elif c == 1:
                parts = head_norm(acc, kgain)
            else:
                parts = [acc[:, hh * HD:(hh + 1) * HD] for hh in range(HEADS)]
            for hh in range(HEADS):
                lo = hh * HD
                if d == 1:
                    a_ref[c, 0, sub_rows, lo:lo + HD] = parts[hh].astype(BF16)
                else:
                    slot = ((g - 1) * 3 + c) * HEADS + hh
                    scr_ref[slot] = parts[hh]
                    for r in range(d):
                        a_ref[c, r, sub_rows, lo:lo + HD] = scr_ref[
                            slot, pl.ds(r, SUB_IN // d, stride=d), :].astype(BF16)

    for t in range(6):
        gt_ref[rows, t * TILE:(t + 1) * TILE] = jax.nn.sigmoid(
            _proj(h_ref, w_ref, rows, T_GATE + t)).astype(BF16)


def _retention(za_ref, decay_ref, zeta_ref, xi_ref, gch_ref, rg_ref, ya_ref, state_ref):
    n_chunks = za_ref.shape[0] // CHUNK
    for hh in range(HEADS):
        cols = lambda t, hh=hh: slice(t * TILE + hh * HD, t * TILE + (hh + 1) * HD)
        decay = decay_ref[hh]
        zeta = zeta_ref[hh]
        xi = xi_ref[hh]
        gch = gch_ref[hh]
        gain = rg_ref[:, hh * HD:(hh + 1) * HD]
        inners, kvs = [], []
        for c in range(n_chunks):
            rows = slice(c * CHUNK, (c + 1) * CHUNK)
            q = za_ref[rows, cols(T_RET_Q)]
            k = za_ref[rows, cols(T_RET_K)]
            v = za_ref[rows, cols(T_RET_V)]
            scores = lax.dot_general(q, k, (((1,), (1,)), ((), ())), preferred_element_type=F32) * decay
            inners.append(jnp.dot(scores.astype(BF16), v, preferred_element_type=F32))
            vz = (v.astype(F32) * zeta).astype(BF16)
            kvs.append(lax.dot_general(k, vz, (((0,), (0,)), ((), ())), preferred_element_type=F32))
        state = state_ref[hh]
        for c in range(n_chunks):
            rows = slice(c * CHUNK, (c + 1) * CHUNK)
            cross = jnp.dot(za_ref[rows, cols(T_RET_Q)], state.astype(BF16),
                            preferred_element_type=F32) * xi
            y = inners[c] + cross
            y = y * lax.rsqrt(jnp.mean(y * y, axis=-1, keepdims=True) + EPS) * gain
            ya_ref[rows, hh * HD:(hh + 1) * HD] = (
                za_ref[rows, cols(T_RET_G)].astype(F32) * y).astype(BF16)
            state = state * gch + kvs[c]
        state_ref[hh] = state


def _spatial_gating(za_ref, ws_ref, bs_ref, yb_ref):
    n_chunks = za_ref.shape[0] // CHUNK
    row = lax.broadcasted_iota(jnp.int32, (CHUNK, CHUNK), 0)
    col = lax.broadcasted_iota(jnp.int32, (CHUNK, CHUNK), 1)
    for g in range(HEADS):
        cols = lambda t, g=g: slice(t * TILE + g * HD, t * TILE + (g + 1) * HD)
        w = jnp.where(row >= col, ws_ref[g], 0.0).astype(BF16)
        bias = bs_ref[g]
        for c in range(n_chunks):
            rows = slice(c * CHUNK, (c + 1) * CHUNK)
            mixed = jnp.dot(w, za_ref[rows, cols(T_SG_V)], preferred_element_type=F32) + bias
            yb_ref[rows, g * HD:(g + 1) * HD] = (
                za_ref[rows, cols(T_SG_U)].astype(F32) * mixed).astype(BF16)


def _inproj(l, x, gmix, w, cos, sin, lng, lnb, qg, kg, decay, zeta, xi, gch, rg, ws, bs):
    s = x.shape[0]
    tm = TM_IN
    row = lambda i: (i, 0)
    table = _resident((HEADS, CHUNK, CHUNK))
    layer_table = _layer((HEADS, CHUNK, CHUNK), l)
    out_shape = [jax.ShapeDtypeStruct((s, BW), BF16)] * 2
    out_specs = [pl.BlockSpec((tm, BW), row)] * 2
    for _, d in DIL_PAIRS:
        out_shape.append(jax.ShapeDtypeStruct((3, d, s // d, BW), BF16))
        out_specs.append(pl.BlockSpec((3, d, tm // d, BW), lambda i: (0, 0, i, 0)))
    out_shape.append(jax.ShapeDtypeStruct((s, N_BRANCH * D_MODEL), BF16))
    out_specs.append(pl.BlockSpec((tm, N_BRANCH * D_MODEL), row))
    return pl.pallas_call(
        _inproj_kernel,
        grid=(s // tm,),
        in_specs=[
            pl.BlockSpec((tm, D_MODEL), row),
            _layer((1, D_MODEL), l),
            _layer((D_MODEL, N_IN), l),
            pl.BlockSpec((tm, HD // 2), row),
            pl.BlockSpec((tm, HD // 2), row),
            _layer((1, BW), l),
            _layer((1, BW), l),
            _layer((1, HD), l),
            _layer((1, HD), l),
            table, table, table, table,
            _layer((1, BW), l), layer_table, layer_table,
        ],
        out_specs=out_specs,
        out_shape=out_shape,
        scratch_shapes=[pltpu.VMEM((tm, D_MODEL), BF16),
                        pltpu.VMEM((tm, N_MIX_TILES * TILE), BF16),
                        pltpu.VMEM((2 * 3 * HEADS, SUB_IN, HD), F32),
                        pltpu.VMEM((HEADS, HD, HD), F32)],
        compiler_params=pltpu.CompilerParams(
            dimension_semantics=("arbitrary",), vmem_limit_bytes=VMEM_LIMIT_V7X),
        name="inproj",
    )(x, gmix, w, cos, sin, lng, lnb, qg, kg, decay, zeta, xi, gch, rg, ws, bs)


def _attn_kernel(bound_ref, q_ref, kp_ref, kc_ref, vp_ref, vc_ref, o_ref, lse_ref):
    tq = q_ref.shape[0]
    first = pl.program_id(1) == 0
    row = lax.broadcasted_iota(jnp.int32, (CHUNK, 2 * CHUNK), 0)
    col = lax.broadcasted_iota(jnp.int32, (CHUNK, 2 * CHUNK), 1)
    band = (col >= row) & (col <= row + CHUNK)
    band_first = band & (col >= jnp.where(first, CHUNK, 0))
    lane = lax.broadcasted_iota(jnp.int32, (CHUNK, HD), 1)
    nt = (((1,), (1,)), ((), ()))
    bound = bound_ref[...]

    def operands(i, hh):
        rows = slice(i * CHUNK, (i + 1) * CHUNK)
        cols = slice(hh * HD, (hh + 1) * HD)
        if i == 0:
            k = jnp.concatenate([kp_ref[:, cols], kc_ref[rows, cols]], axis=0)
            v = jnp.concatenate([vp_ref[:, cols], vc_ref[rows, cols]], axis=0)
        else:
            keys = slice((i - 1) * CHUNK, (i + 1) * CHUNK)
            k = kc_ref[keys, cols]
            v = vc_ref[keys, cols]
        return rows, cols, q_ref[rows, cols], k, v

    shift = jnp.where(band, -bound, NEG_INF)
    shift_first = jnp.where(band_first, -bound, NEG_INF)
    ones = jnp.ones((2 * CHUNK, HD), BF16)
    trusted = jnp.ones((CHUNK, HD), F32)
    for i in range(tq // CHUNK):
        den_tile = jnp.zeros((CHUNK, HD), F32)
        for hh in range(HEADS):
            rows, cols, q, k, v = operands(i, hh)
            s = lax.dot_general(q, k, nt, preferred_element_type=F32)
            p = jnp.exp2(s + (shift_first if i == 0 else shift)).astype(BF16)
            od = jnp.dot(p, jnp.concatenate([v, ones], axis=1), preferred_element_type=F32)
            den = od[:, HD:]
            o_ref[rows, cols] = (od[:, :HD] / den).astype(BF16)
            den_tile = jnp.where(lane // LSE_LANES == hh, den, den_tile)
        lse_ref[i * CHUNK:(i + 1) * CHUNK, :] = (bound[:, :HD] + jnp.log2(den_tile)) * LN2
        trusted = jnp.minimum(trusted, jnp.where(den_tile >= DEN_TRUST_MIN, 1.0, 0.0))

    @pl.when(jnp.min(trusted) < 0.5)
    def _():
        for i in range(tq // CHUNK):
            lse_tile = jnp.zeros((CHUNK, HD), F32)
            for hh in range(HEADS):
                rows, cols, q, k, v = operands(i, hh)
                s = lax.dot_general(q, k, nt, preferred_element_type=F32)
                s = jnp.where(band_first if i == 0 else band, s, NEG_INF)
                m = jnp.max(s, axis=-1, keepdims=True)
                e = jnp.exp2(s - m)
                den = jnp.sum(e, axis=-1, keepdims=True)
                o = jnp.dot(e.astype(BF16), v, preferred_element_type=F32) / den
                o_ref[rows, cols] = o.astype(BF16)
                lse_tile = jnp.where(lane // LSE_LANES == hh, (m + jnp.log2(den)) * LN2, lse_tile)
            lse_ref[i * CHUNK:(i + 1) * CHUNK, :] = lse_tile


def _attention(l, a, d, bound):
    length = a.shape[2]
    tq = TQ_ATT
    assert length % tq == 0
    per = tq // CHUNK
    cur = lambda c: pl.BlockSpec((None, None, tq, BW), lambda r, b, c=c: (c, r, b, 0))
    prev = lambda c: pl.BlockSpec((None, None, CHUNK, BW),
                                  lambda r, b, c=c: (c, r, jnp.maximum(b * per - 1, 0), 0))
    return pl.pallas_call(
        _attn_kernel,
        grid=(d, length // tq),
        in_specs=[_layer((1, 2 * CHUNK), l), cur(0), prev(1), cur(1), prev(2), cur(2)],
        out_specs=[pl.BlockSpec((None, tq, BW), lambda r, b: (r, b, 0)),
                   pl.BlockSpec((None, tq, HD), lambda r, b: (r, b, 0))],
        out_shape=[jax.ShapeDtypeStruct((d, length, BW), BF16),
                   jax.ShapeDtypeStruct((d, length, HD), F32)],
        compiler_params=pltpu.CompilerParams(
            dimension_semantics=("arbitrary", "arbitrary"), vmem_limit_bytes=VMEM_LIMIT_V7X),
        name=f"attn_d{d}",
    )(bound, a, a, a, a, a)


def _merge_ffn_kernel(x_ref, ya_ref, yb_ref, o0_ref, o1_ref, o2_ref, l0_ref, l1_ref, l2_ref, gt_ref,
                      wb_ref, wo_ref, gffn_ref, wgu_ref, wd_ref, out_ref, oscr_ref, lscr_ref, yc_ref):
    tm = x_ref.shape[0]

    def natural_order(src_ref, cols, scr_ref, slot, d):
        if d == 1:
            return src_ref[0, :, cols].astype(F32)
        for r in range(d):
            scr_ref[slot, pl.ds(r, tm // d, stride=d), :] = src_ref[r, :, cols].astype(F32)
        return scr_ref[slot]

    outs, lses = [], []
    for g, (o_ref, l_ref, (_, d)) in enumerate(zip((o0_ref, o1_ref, o2_ref), (l0_ref, l1_ref, l2_ref),
                                                   DIL_PAIRS)):
        outs.append([natural_order(o_ref, slice(hh * HD, (hh + 1) * HD), oscr_ref, g * HEADS + hh, d)
                     for hh in range(HEADS)])
        lses.append(natural_order(l_ref, slice(0, HD), lscr_ref, g, d))
    m = jnp.maximum(jnp.maximum(lses[0], lses[1]), lses[2])
    es = [jnp.exp(l - m) for l in lses]
    inv = 1.0 / (es[0] + es[1] + es[2])
    for hh in range(HEADS):
        cols = slice(hh * HD, (hh + 1) * HD)
        acc = jnp.zeros((tm, HD), F32)
        for g in range(N_DIL):
            wgt = (es[g] * inv)[:, hh * LSE_LANES:hh * LSE_LANES + 1]
            acc = acc + wgt * outs[g][hh]
        yc_ref[:, cols] = acc.astype(BF16)

    merged = jnp.zeros((tm, D_MODEL), F32)
    for b, y in enumerate((ya_ref[...], yb_ref[...], yc_ref[...])):
        gate = gt_ref[:, b * D_MODEL:(b + 1) * D_MODEL].astype(F32)
        merged = merged + gate * jnp.dot(y, wb_ref[b], preferred_element_type=F32)
    x1 = x_ref[...] + jnp.dot(merged.astype(BF16), wo_ref[...], preferred_element_type=F32)

    ms = jnp.mean(x1 * x1, axis=-1, keepdims=True)
    h = (x1 * lax.rsqrt(ms + EPS) * gffn_ref[...]).astype(BF16)
    gate = jnp.dot(h, wgu_ref[:, :D_FF], preferred_element_type=F32)
    up = jnp.dot(h, wgu_ref[:, D_FF:], preferred_element_type=F32)
    act = (jax.nn.silu(gate) * up).astype(BF16)
    out_ref[...] = x1 + jnp.dot(act, wd_ref[...], preferred_element_type=F32)


def _merge_ffn(l, x, ya, yb, outs, lses, gates, wb, wo, gffn, wgu, wd):
    s = x.shape[0]
    tm = TM_OUT
    row = lambda i: (i, 0)
    sub = lambda i: (0, i, 0)
    o_specs = [pl.BlockSpec((d, tm // d, BW), sub) for _, d in DIL_PAIRS]
    l_specs = [pl.BlockSpec((d, tm // d, HD), sub) for _, d in DIL_PAIRS]
    return pl.pallas_call(
        _merge_ffn_kernel,
        grid=(s // tm,),
        in_specs=[pl.BlockSpec((tm, D_MODEL), row),
                  pl.BlockSpec((tm, BW), row), pl.BlockSpec((tm, BW), row)]
                 + o_specs + l_specs
                 + [pl.BlockSpec((tm, N_BRANCH * D_MODEL), row),
                    _layer((N_BRANCH, BW, D_MODEL), l),
                    _layer((D_MODEL, D_MODEL), l),
                    _layer((1, D_MODEL), l),
                    _layer((D_MODEL, 2 * D_FF), l),
                    _layer((D_FF, D_MODEL), l)],
        out_specs=pl.BlockSpec((tm, D_MODEL), row),
        out_shape=jax.ShapeDtypeStruct((s, D_MODEL), F32),
        scratch_shapes=[pltpu.VMEM((N_DIL * HEADS, tm, HD), F32), pltpu.VMEM((N_DIL, tm, HD), F32),
                        pltpu.VMEM((tm, BW), BF16)],
        compiler_params=pltpu.CompilerParams(
            dimension_semantics=("arbitrary",), vmem_limit_bytes=VMEM_LIMIT_V7X),
        name="merge_ffn",
    )(x, ya, yb, *outs, *lses, gates, wb, wo, gffn, wgu, wd)


def _retention_tables():
    c = CHUNK
    log_g = jnp.log1p(-(2.0 ** (-5.0 - jnp.arange(HEADS, dtype=F32))))
    n = jnp.arange(c, dtype=F32)
    diff = n[:, None] - n[None, :]
    scale = HD ** -0.5
    decay = jnp.where(diff >= 0, jnp.exp(log_g[:, None, None] * jnp.maximum(diff, 0.0)), 0.0) * scale
    zeta = jnp.exp(log_g[:, None] * (c - 1.0 - n)[None, :]) * scale
    xi = jnp.exp(log_g[:, None] * (n + 1.0)[None, :])
    g_chunk = jnp.exp(log_g * c)
    full = lambda t: jnp.broadcast_to(t, (HEADS, c, c)).astype(F32)
    return decay, full(zeta[:, :, None]), full(xi[:, :, None]), full(g_chunk[:, None, None])


def _rotary_tables(s):
    half = HD // 2
    inv = 1.0 / (ROPE_BASE ** (jnp.arange(half, dtype=F32) / half))
    ang = jnp.arange(s, dtype=jnp.int32).astype(F32)[:, None] * inv[None, :]
    return jnp.cos(ang), jnp.sin(ang)


def kernel(x, g_mix, w_in, ret_norm_g, sg_ln_g, sg_ln_b, sg_w, sg_b, dil_q_norm_g, dil_k_norm_g,
           w_branch, w_o, g_ffn, w_gate_up, w_down):
    batch, s, d_model = x.shape
    depth = w_in.shape[0]
    assert batch == 1 and d_model == D_MODEL and w_in.shape[-1] == N_IN
    assert s % (DIL_PAIRS[-1][1] * CHUNK) == 0 and s % TM_IN == 0 and s % TM_OUT == 0
    for win, d in DIL_PAIRS:
        assert win // d == CHUNK and (s // d) % TQ_ATT == 0
        assert SUB_IN % (16 * d) == 0 and TM_IN % SUB_IN == 0 and TM_OUT % (16 * d) == 0

    w_in_b = w_in.astype(BF16)
    w_branch_b = w_branch.astype(BF16)
    w_o_b = w_o.astype(BF16)
    w_gu_b = w_gate_up.astype(BF16)
    w_d_b = w_down.astype(BF16)
    cos, sin = _rotary_tables(s)
    decay, zeta, xi, gch = _retention_tables()
    bias_s = jnp.broadcast_to(sg_b[:, :, :, None], sg_b.shape + (CHUNK,)).astype(F32)
    vec = lambda p: p[:, None, :]
    score_bound = (LOG2E * HD ** 0.5) * (jnp.max(jnp.abs(dil_q_norm_g), axis=-1)
                                         * jnp.max(jnp.abs(dil_k_norm_g), axis=-1))
    score_bound = jnp.broadcast_to(score_bound[:, None, None], (depth, 1, 2 * CHUNK)).astype(F32)

    xs = x.reshape(s, d_model)
    for l in range(depth):
        ya, yb, a0, a1, a2, gates = _inproj(
            l, xs, vec(g_mix), w_in_b, cos, sin, vec(sg_ln_g), vec(sg_ln_b),
            vec(dil_q_norm_g), vec(dil_k_norm_g), decay, zeta, xi, gch, vec(ret_norm_g), sg_w, bias_s)
        outs, lses = [], []
        for a, (_, d) in zip((a0, a1, a2), DIL_PAIRS):
            o, lse = _attention(l, a, d, score_bound)
            outs.append(o)
            lses.append(lse)
        xs = _merge_ffn(l, xs, ya, yb, outs, lses, gates, w_branch_b, w_o_b, vec(g_ffn),
                        w_gu_b, w_d_b)
    return xs.reshape(batch, s, d_model)
```

```python
import functools

import jax
import jax.numpy as jnp
from jax import lax
from jax.experimental import pallas as pl
from jax.experimental.pallas import tpu as pltpu

F32 = jnp.float32
BF16 = jnp.bfloat16

D_MODEL = 1024
EPS = 1e-6
NEG_INF = -1e30
HEADS = 4
HD = 128
CHUNK = 128
ROPE_BASE = 10000.0
DIL_PAIRS = ((128, 1), (512, 4), (2048, 16))
N_DIL = len(DIL_PAIRS)
BW = HEADS * HD
N_BRANCH = 3
D_FF = 2816
N_IN = 10752
TILE = 512
LSE_LANES = HD // HEADS
LOG2E = 1.4426950408889634
LN2 = 0.6931471805599453
DEN_TRUST_MIN = 1e-30

T_RET_Q, T_RET_K, T_RET_V, T_RET_G, T_SG_U, T_SG_V = range(6)
T_DIL_Q, T_DIL_K, T_DIL_V = 6, 6 + N_DIL, 6 + 2 * N_DIL
T_GATE = 6 + 3 * N_DIL
N_MIX_TILES = 6

VMEM_LIMIT_V7X = 56 * 1024 * 1024

TM_IN = 512
SUB_IN = 256
MIX_EVERY = 2
TQ_ATT = 1024
TM_OUT = 512


def _resident(shape):
    nd = len(shape)
    return pl.BlockSpec(shape, lambda *_: (0,) * nd, pipeline_mode=pl.Buffered(1))


def _layer(shape, l):
    nd = len(shape)
    return pl.BlockSpec((None,) + tuple(shape), lambda *_: (l,) + (0,) * nd,
                        pipeline_mode=pl.Buffered(1))


def _inproj_kernel(x_ref, gmix_ref, w_ref, cos_ref, sin_ref, lng_ref, lnb_ref, qg_ref, kg_ref,
                   decay_ref, zeta_ref, xi_ref, gch_ref, rg_ref, ws_ref, bs_ref,
                   ya_ref, yb_ref, a0_ref, a1_ref, a2_ref, gt_ref,
                   h_ref, za_ref, scr_ref, state_ref):
    n_sub = x_ref.shape[0] // SUB_IN

    @pl.when(pl.program_id(0) == 0)
    def _():
        state_ref[...] = jnp.zeros_like(state_ref)

    for sb in range(n_sub):
        _project_mixer_inputs(sb, x_ref, gmix_ref, w_ref, cos_ref, sin_ref, lng_ref, lnb_ref,
                              h_ref, za_ref)
    mixers = (_retention_heads(za_ref, decay_ref, zeta_ref, xi_ref, gch_ref, rg_ref, ya_ref, state_ref)
              + _spatial_gating_groups(za_ref, ws_ref, bs_ref, yb_ref))
    tiles = []
    for sb in range(n_sub):
        tiles += _attention_and_gate_tiles(sb, w_ref, qg_ref, kg_ref, h_ref, (a0_ref, a1_ref, a2_ref),
                                           gt_ref, scr_ref)
    for i, tile in enumerate(tiles):
        tile()
        if i % MIX_EVERY == MIX_EVERY - 1 and i // MIX_EVERY < len(mixers):
            mixers[i // MIX_EVERY]()
    assert len(tiles) // MIX_EVERY >= len(mixers)


def _proj(h_ref, w_ref, rows, t):
    return jnp.dot(h_ref[rows, :], w_ref[:, t * TILE:(t + 1) * TILE], preferred_element_type=F32)


def _project_mixer_inputs(sb, x_ref, gmix_ref, w_ref, cos_ref, sin_ref, lng_ref, lnb_ref,
                          h_ref, za_ref):
    rows = slice(sb * SUB_IN, (sb + 1) * SUB_IN)
    x = x_ref[rows, :]
    ms = jnp.mean(x * x, axis=-1, keepdims=True)
    h_ref[rows, :] = (x * lax.rsqrt(ms + EPS) * gmix_ref[...]).astype(BF16)

    c = cos_ref[rows, :]
    s = sin_ref[rows, :]
    cos = jnp.concatenate([c, c], axis=1)
    sin = jnp.concatenate([-s, s], axis=1)
    for t in (T_RET_Q, T_RET_K):
        acc = _proj(h_ref, w_ref, rows, t)
        for hh in range(HEADS):
            a = acc[:, hh * HD:(hh + 1) * HD]
            za_ref[rows, t * TILE + hh * HD:t * TILE + (hh + 1) * HD] = (
                a * cos + pltpu.roll(a, HD // 2, 1) * sin).astype(BF16)
    tile = lambda t: slice(t * TILE, (t + 1) * TILE)
    za_ref[rows, tile(T_RET_V)] = _proj(h_ref, w_ref, rows, T_RET_V).astype(BF16)
    za_ref[rows, tile(T_RET_G)] = jax.nn.silu(_proj(h_ref, w_ref, rows, T_RET_G)).astype(BF16)
    za_ref[rows, tile(T_SG_U)] = jax.nn.gelu(_proj(h_ref, w_ref, rows, T_SG_U)).astype(BF16)
    v = jax.nn.gelu(_proj(h_ref, w_ref, rows, T_SG_V))
    mu = jnp.mean(v, axis=-1, keepdims=True)
    vc = v - mu
    var = jnp.mean(vc * vc, axis=-1, keepdims=True)
    za_ref[rows, tile(T_SG_V)] = (vc * lax.rsqrt(var + EPS) * lng_ref[...] + lnb_ref[...]).astype(BF16)


def _attention_and_gate_tiles(sb, w_ref, qg_ref, kg_ref, h_ref, a_refs, gt_ref, scr_ref):
    rows = slice(sb * SUB_IN, (sb + 1) * SUB_IN)

    def head_norm(acc, gain):
        outs = []
        for hh in range(HEADS):
            a = acc[:, hh * HD:(hh + 1) * HD]
            outs.append(a * lax.rsqrt(jnp.mean(a * a, axis=-1, keepdims=True) + EPS) * gain)
        return outs

    def attention_tile(g, c):
        a_ref, (_, d) = a_refs[g], DIL_PAIRS[g]
        sub_rows = slice(sb * SUB_IN // d, (sb + 1) * SUB_IN // d)
        acc = _proj(h_ref, w_ref, rows, (T_DIL_Q, T_DIL_K, T_DIL_V)[c] + g)
        if c == 0:
            parts = head_norm(acc, qg_ref[...] * (LOG2E * HD ** -0.5))
        elif c == 1:
            parts = head_norm(acc, kg_ref[...])
        else:
            parts = [acc[:, hh * HD:(hh + 1) * HD] for hh in range(HEADS)]
        for hh in range(HEADS):
            lo = hh * HD
            if d == 1:
                a_ref[c, 0, sub_rows, lo:lo + HD] = parts[hh].astype(BF16)
            else:
                slot = ((g - 1) * 3 + c) * HEADS + hh
                scr_ref[slot] = parts[hh]
                for r in range(d):
                    a_ref[c, r, sub_rows, lo:lo + HD] = scr_ref[
                        slot, pl.ds(r, SUB_IN // d, stride=d), :].astype(BF16)

    def gate_tile(t):
        gt_ref[rows, t * TILE:(t + 1) * TILE] = jax.nn.sigmoid(
            _proj(h_ref, w_ref, rows, T_GATE + t)).astype(BF16)

    thunks = [lambda g=g, c=c: attention_tile(g, c) for g in range(N_DIL) for c in range(3)]
    return thunks + [lambda t=t: gate_tile(t) for t in range(N_BRANCH * D_MODEL // TILE)]


def _retention_heads(za_ref, decay_ref, zeta_ref, xi_ref, gch_ref, rg_ref, ya_ref, state_ref):
    n_chunks = za_ref.shape[0] // CHUNK

    def head(hh):
        cols = lambda t: slice(t * TILE + hh * HD, t * TILE + (hh + 1) * HD)
        decay = decay_ref[hh]
        zeta = zeta_ref[hh]
        xi = xi_ref[hh]
        gch = gch_ref[hh]
        gain = rg_ref[:, hh * HD:(hh + 1) * HD]
        inners, kvs = [], []
        for c in range(n_chunks):
            rows = slice(c * CHUNK, (c + 1) * CHUNK)
            q = za_ref[rows, cols(T_RET_Q)]
            k = za_ref[rows, cols(T_RET_K)]
            v = za_ref[rows, cols(T_RET_V)]
            scores = lax.dot_general(q, k, (((1,), (1,)), ((), ())), preferred_element_type=F32) * decay
            inners.append(jnp.dot(scores.astype(BF16), v, preferred_element_type=F32))
            vz = (v.astype(F32) * zeta).astype(BF16)
            kvs.append(lax.dot_general(k, vz, (((0,), (0,)), ((), ())), preferred_element_type=F32))
        state = state_ref[hh]
        for c in range(n_chunks):
            rows = slice(c * CHUNK, (c + 1) * CHUNK)
            cross = jnp.dot(za_ref[rows, cols(T_RET_Q)], state.astype(BF16),
                            preferred_element_type=F32) * xi
            y = inners[c] + cross
            y = y * lax.rsqrt(jnp.mean(y * y, axis=-1, keepdims=True) + EPS) * gain
            ya_ref[rows, hh * HD:(hh + 1) * HD] = (
                za_ref[rows, cols(T_RET_G)].astype(F32) * y).astype(BF16)
            state = state * gch + kvs[c]
        state_ref[hh] = state

    return [lambda hh=hh: head(hh) for hh in range(HEADS)]


def _spatial_gating_groups(za_ref, ws_ref, bs_ref, yb_ref):
    n_chunks = za_ref.shape[0] // CHUNK

    def group(g):
        row = lax.broadcasted_iota(jnp.int32, (CHUNK, CHUNK), 0)
        col = lax.broadcasted_iota(jnp.int32, (CHUNK, CHUNK), 1)
        cols = lambda t: slice(t * TILE + g * HD, t * TILE + (g + 1) * HD)
        w = jnp.where(row >= col, ws_ref[g], 0.0).astype(BF16)
        bias = bs_ref[g]
        v_all = jnp.concatenate(
            [za_ref[c * CHUNK:(c + 1) * CHUNK, cols(T_SG_V)] for c in range(n_chunks)], axis=1)
        mixed_all = jnp.dot(w, v_all, preferred_element_type=F32)
        for c in range(n_chunks):
            rows = slice(c * CHUNK, (c + 1) * CHUNK)
            mixed = mixed_all[:, c * HD:(c + 1) * HD] + bias
            yb_ref[rows, g * HD:(g + 1) * HD] = (
                za_ref[rows, cols(T_SG_U)].astype(F32) * mixed).astype(BF16)

    return [lambda g=g: group(g) for g in range(HEADS)]


def _inproj(l, x, gmix, w, cos, sin, lng, lnb, qg, kg, decay, zeta, xi, gch, rg, ws, bs):
    s = x.shape[0]
    tm = TM_IN
    row = lambda i: (i, 0)
    table = _resident((HEADS, CHUNK, CHUNK))
    layer_table = _layer((HEADS, CHUNK, CHUNK), l)
    out_shape = [jax.ShapeDtypeStruct((s, BW), BF16)] * 2
    out_specs = [pl.BlockSpec((tm, BW), row)] * 2
    for _, d in DIL_PAIRS:
        out_shape.append(jax.ShapeDtypeStruct((3, d, s // d, BW), BF16))
        out_specs.append(pl.BlockSpec((3, d, tm // d, BW), lambda i: (0, 0, i, 0)))
    out_shape.append(jax.ShapeDtypeStruct((s, N_BRANCH * D_MODEL), BF16))
    out_specs.append(pl.BlockSpec((tm, N_BRANCH * D_MODEL), row))
    return pl.pallas_call(
        _inproj_kernel,
        grid=(s // tm,),
        in_specs=[
            pl.BlockSpec((tm, D_MODEL), row),
            _layer((1, D_MODEL), l),
            _resident((D_MODEL, N_IN)),
            pl.BlockSpec((tm, HD // 2), row),
            pl.BlockSpec((tm, HD // 2), row),
            _layer((1, BW), l),
            _layer((1, BW), l),
            _layer((1, HD), l),
            _layer((1, HD), l),
            table, table, table, table,
            _layer((1, BW), l), layer_table, layer_table,
        ],
        out_specs=out_specs,
        out_shape=out_shape,
        scratch_shapes=[pltpu.VMEM((tm, D_MODEL), BF16),
                        pltpu.VMEM((tm, N_MIX_TILES * TILE), BF16),
                        pltpu.VMEM((2 * 3 * HEADS, SUB_IN, HD), F32),
                        pltpu.VMEM((HEADS, HD, HD), F32)],
        compiler_params=pltpu.CompilerParams(
            dimension_semantics=("arbitrary",), vmem_limit_bytes=VMEM_LIMIT_V7X),
        name="inproj",
    )(x, gmix, w, cos, sin, lng, lnb, qg, kg, decay, zeta, xi, gch, rg, ws, bs)


def _attn_kernel(n_cast, bound_ref, q_ref, kp_ref, kc_ref, vp_ref, vc_ref, *refs):
    w32_refs = refs[:n_cast]
    o_ref, lse_ref = refs[n_cast:n_cast + 2]
    w16_refs = refs[n_cast + 2:]
    for w32_ref, w16_ref in zip(w32_refs, w16_refs):
        w16_ref[...] = w32_ref[...].astype(BF16)

    tq = q_ref.shape[0]
    first = pl.program_id(1) == 0
    row = lax.broadcasted_iota(jnp.int32, (CHUNK, 2 * CHUNK), 0)
    col = lax.broadcasted_iota(jnp.int32, (CHUNK, 2 * CHUNK), 1)
    band = (col >= row) & (col <= row + CHUNK)
    band_first = band & (col >= jnp.where(first, CHUNK, 0))
    lane = lax.broadcasted_iota(jnp.int32, (CHUNK, HD), 1)
    nt = (((1,), (1,)), ((), ()))
    bound = bound_ref[...]

    def operands(i, hh):
        rows = slice(i * CHUNK, (i + 1) * CHUNK)
        cols = slice(hh * HD, (hh + 1) * HD)
        if i == 0:
            k = jnp.concatenate([kp_ref[:, cols], kc_ref[rows, cols]], axis=0)
            v = jnp.concatenate([vp_ref[:, cols], vc_ref[rows, cols]], axis=0)
        else:
            keys = slice((i - 1) * CHUNK, (i + 1) * CHUNK)
            k = kc_ref[keys, cols]
            v = vc_ref[keys, cols]
        return rows, cols, q_ref[rows, cols], k, v

    shift = jnp.where(band, -bound, NEG_INF)
    shift_first = jnp.where(band_first, -bound, NEG_INF)
    ones = jnp.ones((2 * CHUNK, HD), BF16)
    trusted = jnp.ones((CHUNK, HD), F32)
    for i in range(tq // CHUNK):
        den_tile = jnp.zeros((CHUNK, HD), F32)
        for hh in range(HEADS):
            rows, cols, q, k, v = operands(i, hh)
            s = lax.dot_general(q, k, nt, preferred_element_type=F32)
            p = jnp.exp2(s + (shift_first if i == 0 else shift)).astype(BF16)
            od = jnp.dot(p, jnp.concatenate([v, ones], axis=1), preferred_element_type=F32)
            den = od[:, HD:]
            o_ref[rows, cols] = (od[:, :HD] / den).astype(BF16)
            den_tile = jnp.where(lane // LSE_LANES == hh, den, den_tile)
        lse_ref[i * CHUNK:(i + 1) * CHUNK, :] = (bound[:, :HD] + jnp.log2(den_tile)) * LN2
        trusted = jnp.minimum(trusted, jnp.where(den_tile >= DEN_TRUST_MIN, 1.0, 0.0))

    @pl.when(jnp.min(trusted) < 0.5)
    def _():
        for i in range(tq // CHUNK):
            lse_tile = jnp.zeros((CHUNK, HD), F32)
            for hh in range(HEADS):
                rows, cols, q, k, v = operands(i, hh)
                s = lax.dot_general(q, k, nt, preferred_element_type=F32)
                s = jnp.where(band_first if i == 0 else band, s, NEG_INF)
                m = jnp.max(s, axis=-1, keepdims=True)
                e = jnp.exp2(s - m)
                den = jnp.sum(e, axis=-1, keepdims=True)
                o = jnp.dot(e.astype(BF16), v, preferred_element_type=F32) / den
                o_ref[rows, cols] = o.astype(BF16)
                lse_tile = jnp.where(lane // LSE_LANES == hh, (m + jnp.log2(den)) * LN2, lse_tile)
            lse_ref[i * CHUNK:(i + 1) * CHUNK, :] = lse_tile


def _attention(l, a, d, bound, casts=()):
    length = a.shape[2]
    tq = TQ_ATT
    assert length % tq == 0
    per = tq // CHUNK
    nb = length // tq
    n_steps = d * nb
    cur = lambda c: pl.BlockSpec((None, None, tq, BW), lambda r, b, c=c: (c, r, b, 0))
    prev = lambda c: pl.BlockSpec((None, None, CHUNK, BW),
                                  lambda r, b, c=c: (c, r, jnp.maximum(b * per - 1, 0), 0))
    cast_in, cast_out, cast_shape = [], [], []
    for w, wl in casts:
        _, rows, cols = w.shape
        rt = rows // n_steps
        assert rows % n_steps == 0 and rt % 16 == 0
        cast_in.append(pl.BlockSpec((None, rt, cols), lambda r, b, wl=wl: (wl, r * nb + b, 0)))
        cast_out.append(pl.BlockSpec((rt, cols), lambda r, b: (r * nb + b, 0)))
        cast_shape.append(jax.ShapeDtypeStruct((rows, cols), BF16))
    return pl.pallas_call(
        functools.partial(_attn_kernel, len(casts)),
        grid=(d, nb),
        in_specs=[_layer((1, 2 * CHUNK), l), cur(0), prev(1), cur(1), prev(2), cur(2)] + cast_in,
        out_specs=[pl.BlockSpec((None, tq, BW), lambda r, b: (r, b, 0)),
                   pl.BlockSpec((None, tq, HD), lambda r, b: (r, b, 0))] + cast_out,
        out_shape=[jax.ShapeDtypeStruct((d, length, BW), BF16),
                   jax.ShapeDtypeStruct((d, length, HD), F32)] + cast_shape,
        compiler_params=pltpu.CompilerParams(
            dimension_semantics=("arbitrary", "arbitrary"), vmem_limit_bytes=VMEM_LIMIT_V7X),
        name=f"attn_d{d}",
    )(bound, a, a, a, a, a, *[w for w, _ in casts])


def _merge_ffn_kernel(x_ref, ya_ref, yb_ref, o0_ref, o1_ref, o2_ref, l0_ref, l1_ref, l2_ref, gt_ref,
                      wb_ref, wo_ref, gffn_ref, wgu_ref, wd_ref, out_ref, oscr_ref, lscr_ref, yc_ref):
    tm = x_ref.shape[0]

    def natural_order(src_ref, cols, scr_ref, slot, d):
        if d == 1:
            return src_ref[0, :, cols].astype(F32)
        for r in range(d):
            scr_ref[slot, pl.ds(r, tm // d, stride=d), :] = src_ref[r, :, cols].astype(F32)
        return scr_ref[slot]

    outs, lses = [], []
    for g, (o_ref, l_ref, (_, d)) in enumerate(zip((o0_ref, o1_ref, o2_ref), (l0_ref, l1_ref, l2_ref),
                                                   DIL_PAIRS)):
        outs.append([natural_order(o_ref, slice(hh * HD, (hh + 1) * HD), oscr_ref, g * HEADS + hh, d)
                     for hh in range(HEADS)])
        lses.append(natural_order(l_ref, slice(0, HD), lscr_ref, g, d))
    m = jnp.maximum(jnp.maximum(lses[0], lses[1]), lses[2])
    es = [jnp.exp(l - m) for l in lses]
    inv = 1.0 / (es[0] + es[1] + es[2])
    for hh in range(HEADS):
        cols = slice(hh * HD, (hh + 1) * HD)
        acc = jnp.zeros((tm, HD), F32)
        for g in range(N_DIL):
            wgt = (es[g] * inv)[:, hh * LSE_LANES:hh * LSE_LANES + 1]
            acc = acc + wgt * outs[g][hh]
        yc_ref[:, cols] = acc.astype(BF16)

    merged = jnp.zeros((tm, D_MODEL), F32)
    for b, y in enumerate((ya_ref[...], yb_ref[...], yc_ref[...])):
        gate = gt_ref[:, b * D_MODEL:(b + 1) * D_MODEL].astype(F32)
        merged = merged + gate * jnp.dot(y, wb_ref[b * BW:(b + 1) * BW, :], preferred_element_type=F32)
    x1 = x_ref[...] + jnp.dot(merged.astype(BF16), wo_ref[...], preferred_element_type=F32)

    ms = jnp.mean(x1 * x1, axis=-1, keepdims=True)
    h = (x1 * lax.rsqrt(ms + EPS) * gffn_ref[...]).astype(BF16)
    gate = jnp.dot(h, wgu_ref[:, :D_FF], preferred_element_type=F32)
    up = jnp.dot(h, wgu_ref[:, D_FF:], preferred_element_type=F32)
    act = (jax.nn.silu(gate) * up).astype(BF16)
    out_ref[...] = x1 + jnp.dot(act, wd_ref[...], preferred_element_type=F32)


def _merge_ffn(l, x, ya, yb, outs, lses, gates, wb, wo, gffn, wgu, wd):
    s = x.shape[0]
    tm = TM_OUT
    row = lambda i: (i, 0)
    sub = lambda i: (0, i, 0)
    o_specs = [pl.BlockSpec((d, tm // d, BW), sub) for _, d in DIL_PAIRS]
    l_specs = [pl.BlockSpec((d, tm // d, HD), sub) for _, d in DIL_PAIRS]
    return pl.pallas_call(
        _merge_ffn_kernel,
        grid=(s // tm,),
        in_specs=[pl.BlockSpec((tm, D_MODEL), row),
                  pl.BlockSpec((tm, BW), row), pl.BlockSpec((tm, BW), row)]
                 + o_specs + l_specs
                 + [pl.BlockSpec((tm, N_BRANCH * D_MODEL), row),
                    _resident((N_BRANCH * BW, D_MODEL)),
                    _resident((D_MODEL, D_MODEL)),
                    _layer((1, D_MODEL), l),
                    _resident((D_MODEL, 2 * D_FF)),
                    _resident((D_FF, D_MODEL))],
        out_specs=pl.BlockSpec((tm, D_MODEL), row),
        out_shape=jax.ShapeDtypeStruct((s, D_MODEL), F32),
        scratch_shapes=[pltpu.VMEM((N_DIL * HEADS, tm, HD), F32), pltpu.VMEM((N_DIL, tm, HD), F32),
                        pltpu.VMEM((tm, BW), BF16)],
        compiler_params=pltpu.CompilerParams(
            dimension_semantics=("arbitrary",), vmem_limit_bytes=VMEM_LIMIT_V7X),
        name="merge_ffn",
    )(x, ya, yb, *outs, *lses, gates, wb, wo, gffn, wgu, wd)


def _retention_tables():
    c = CHUNK
    log_g = jnp.log1p(-(2.0 ** (-5.0 - jnp.arange(HEADS, dtype=F32))))
    n = jnp.arange(c, dtype=F32)
    diff = n[:, None] - n[None, :]
    scale = HD ** -0.5
    decay = jnp.where(diff >= 0, jnp.exp(log_g[:, None, None] * jnp.maximum(diff, 0.0)), 0.0) * scale
    zeta = jnp.exp(log_g[:, None] * (c - 1.0 - n)[None, :]) * scale
    xi = jnp.exp(log_g[:, None] * (n + 1.0)[None, :])
    g_chunk = jnp.exp(log_g * c)
    full = lambda t: jnp.broadcast_to(t, (HEADS, c, c)).astype(F32)
    return decay, full(zeta[:, :, None]), full(xi[:, :, None]), full(g_chunk[:, None, None])


def _rotary_tables(s):
    half = HD // 2
    inv = 1.0 / (ROPE_BASE ** (jnp.arange(half, dtype=F32) / half))
    ang = jnp.arange(s, dtype=jnp.int32).astype(F32)[:, None] * inv[None, :]
    return jnp.cos(ang), jnp.sin(ang)


def kernel(x, g_mix, w_in, ret_norm_g, sg_ln_g, sg_ln_b, sg_w, sg_b, dil_q_norm_g, dil_k_norm_g,
           w_branch, w_o, g_ffn, w_gate_up, w_down):
    batch, s, d_model = x.shape
    depth = w_in.shape[0]
    assert batch == 1 and d_model == D_MODEL and w_in.shape[-1] == N_IN
    assert s % (DIL_PAIRS[-1][1] * CHUNK) == 0 and s % TM_IN == 0 and s % TM_OUT == 0
    for win, d in DIL_PAIRS:
        assert win // d == CHUNK and (s // d) % TQ_ATT == 0
        assert SUB_IN % (16 * d) == 0 and TM_IN % SUB_IN == 0 and TM_OUT % (16 * d) == 0

    w_in_l = w_in[0].astype(BF16)
    w_branch_rows = w_branch.reshape(depth, N_BRANCH * BW, D_MODEL)
    cos, sin = _rotary_tables(s)
    decay, zeta, xi, gch = _retention_tables()
    bias_s = jnp.broadcast_to(sg_b[:, :, :, None], sg_b.shape + (CHUNK,)).astype(F32)
    vec = lambda p: p[:, None, :]
    score_bound = (LOG2E * HD ** 0.5) * (jnp.max(jnp.abs(dil_q_norm_g), axis=-1)
                                         * jnp.max(jnp.abs(dil_k_norm_g), axis=-1))
    score_bound = jnp.broadcast_to(score_bound[:, None, None], (depth, 1, 2 * CHUNK)).astype(F32)

    xs = x.reshape(s, d_model)
    for l in range(depth):
        ya, yb, a0, a1, a2, gates = _inproj(
            l, xs, vec(g_mix), w_in_l, cos, sin, vec(sg_ln_g), vec(sg_ln_b),
            vec(dil_q_norm_g), vec(dil_k_norm_g), decay, zeta, xi, gch, vec(ret_norm_g), sg_w, bias_s)
        casts = ([(w_gate_up, l)],
                 [(w_down, l), (w_branch_rows, l), (w_o, l)],
                 [(w_in, l + 1)] if l + 1 < depth else [])
        outs, lses, w16 = [], [], []
        for a, (_, d), cast in zip((a0, a1, a2), DIL_PAIRS, casts):
            o, lse, *copies = _attention(l, a, d, score_bound, cast)
            outs.append(o)
            lses.append(lse)
            w16 += copies
        w_gu_l, w_d_l, w_b_l, w_o_l = w16[:4]
        xs = _merge_ffn(l, xs, ya, yb, outs, lses, gates, w_b_l, w_o_l, vec(g_ffn), w_gu_l, w_d_l)
        if l + 1 < depth:
            w_in_l = w16[4]
    return xs.reshape(batch, s, d_model)
```

```python
import functools

import jax
import jax.numpy as jnp
from jax import lax
from jax.experimental import pallas as pl
from jax.experimental.pallas import tpu as pltpu

F32 = jnp.float32
BF16 = jnp.bfloat16

D_MODEL = 1024
EPS = 1e-6
NEG_INF = -1e30
HEADS = 4
HD = 128
CHUNK = 128
ROPE_BASE = 10000.0
DIL_PAIRS = ((128, 1), (512, 4), (2048, 16))
N_DIL = len(DIL_PAIRS)
BW = HEADS * HD
N_BRANCH = 3
D_FF = 2816
N_IN = 10752
TILE = 512
LSE_LANES = HD // HEADS
LOG2E = 1.4426950408889634
LN2 = 0.6931471805599453
DEN_TRUST_MIN = 1e-30

T_RET_Q, T_RET_K, T_RET_V, T_RET_G, T_SG_U, T_SG_V = range(6)
T_DIL_Q, T_DIL_K, T_DIL_V = 6, 6 + N_DIL, 6 + 2 * N_DIL
T_GATE = 6 + 3 * N_DIL
N_MIX_TILES = 6

VMEM_LIMIT_V7X = 56 * 1024 * 1024

TM_IN = 512
SUB_IN = 256
MIX_EVERY = 2
TQ_ATT = 1024
TM_OUT = 512


def _resident(shape):
    nd = len(shape)
    return pl.BlockSpec(shape, lambda *_: (0,) * nd, pipeline_mode=pl.Buffered(1))


def _layer(shape, l):
    nd = len(shape)
    return pl.BlockSpec((None,) + tuple(shape), lambda *_: (l,) + (0,) * nd,
                        pipeline_mode=pl.Buffered(1))


def _inproj_kernel(x_ref, gmix_ref, w_ref, cos_ref, sin_ref, lng_ref, lnb_ref, qg_ref, kg_ref,
                   decay_ref, zeta_ref, xi_ref, gch_ref, rg_ref, ws_ref, bs_ref, wgu32_ref,
                   ya_ref, yb_ref, a0_ref, a1_ref, a2_ref, gt_ref, wgu16_ref,
                   h_ref, za_ref, scr_ref, state_ref):
    n_sub = x_ref.shape[0] // SUB_IN
    wgu16_ref[...] = wgu32_ref[...].astype(BF16)

    @pl.when(pl.program_id(0) == 0)
    def _():
        state_ref[...] = jnp.zeros_like(state_ref)

    for sb in range(n_sub):
        _project_mixer_inputs(sb, x_ref, gmix_ref, w_ref, cos_ref, sin_ref, lng_ref, lnb_ref,
                              h_ref, za_ref)
    mixers = (_retention_heads(za_ref, decay_ref, zeta_ref, xi_ref, gch_ref, rg_ref, ya_ref, state_ref)
              + _spatial_gating_groups(za_ref, ws_ref, bs_ref, yb_ref))
    tiles = []
    for sb in range(n_sub):
        tiles += _attention_and_gate_tiles(sb, w_ref, qg_ref, kg_ref, h_ref, (a0_ref, a1_ref, a2_ref),
                                           gt_ref, scr_ref)
    for i, tile in enumerate(tiles):
        tile()
        if i % MIX_EVERY == MIX_EVERY - 1 and i // MIX_EVERY < len(mixers):
            mixers[i // MIX_EVERY]()
    assert len(tiles) // MIX_EVERY >= len(mixers)


def _proj(h_ref, w_ref, rows, t):
    return jnp.dot(h_ref[rows, :], w_ref[:, t * TILE:(t + 1) * TILE], preferred_element_type=F32)


def _project_mixer_inputs(sb, x_ref, gmix_ref, w_ref, cos_ref, sin_ref, lng_ref, lnb_ref,
                          h_ref, za_ref):
    rows = slice(sb * SUB_IN, (sb + 1) * SUB_IN)
    x = x_ref[rows, :]
    ms = jnp.mean(x * x, axis=-1, keepdims=True)
    h_ref[rows, :] = (x * lax.rsqrt(ms + EPS) * gmix_ref[...]).astype(BF16)

    c = cos_ref[rows, :]
    s = sin_ref[rows, :]
    cos = jnp.concatenate([c, c], axis=1)
    sin = jnp.concatenate([-s, s], axis=1)
    for t in (T_RET_Q, T_RET_K):
        acc = _proj(h_ref, w_ref, rows, t)
        for hh in range(HEADS):
            a = acc[:, hh * HD:(hh + 1) * HD]
            za_ref[rows, t * TILE + hh * HD:t * TILE + (hh + 1) * HD] = (
                a * cos + pltpu.roll(a, HD // 2, 1) * sin).astype(BF16)
    tile = lambda t: slice(t * TILE, (t + 1) * TILE)
    za_ref[rows, tile(T_RET_V)] = _proj(h_ref, w_ref, rows, T_RET_V).astype(BF16)
    za_ref[rows, tile(T_RET_G)] = jax.nn.silu(_proj(h_ref, w_ref, rows, T_RET_G)).astype(BF16)
    za_ref[rows, tile(T_SG_U)] = jax.nn.gelu(_proj(h_ref, w_ref, rows, T_SG_U)).astype(BF16)
    v = jax.nn.gelu(_proj(h_ref, w_ref, rows, T_SG_V))
    mu = jnp.mean(v, axis=-1, keepdims=True)
    vc = v - mu
    var = jnp.mean(vc * vc, axis=-1, keepdims=True)
    za_ref[rows, tile(T_SG_V)] = (vc * lax.rsqrt(var + EPS) * lng_ref[...] + lnb_ref[...]).astype(BF16)


def _attention_and_gate_tiles(sb, w_ref, qg_ref, kg_ref, h_ref, a_refs, gt_ref, scr_ref):
    rows = slice(sb * SUB_IN, (sb + 1) * SUB_IN)

    def head_norm(acc, gain):
        outs = []
        for hh in range(HEADS):
            a = acc[:, hh * HD:(hh + 1) * HD]
            outs.append(a * lax.rsqrt(jnp.mean(a * a, axis=-1, keepdims=True) + EPS) * gain)
        return outs

    def attention_tile(g, c):
        a_ref, (_, d) = a_refs[g], DIL_PAIRS[g]
        sub_rows = slice(sb * SUB_IN // d, (sb + 1) * SUB_IN // d)
        acc = _proj(h_ref, w_ref, rows, (T_DIL_Q, T_DIL_K, T_DIL_V)[c] + g)
        if c == 0:
            parts = head_norm(acc, qg_ref[...] * (LOG2E * HD ** -0.5))
        elif c == 1:
            parts = head_norm(acc, kg_ref[...])
        else:
            parts = [acc[:, hh * HD:(hh + 1) * HD] for hh in range(HEADS)]
        for hh in range(HEADS):
            lo = hh * HD
            if d == 1:
                a_ref[c, 0, sub_rows, lo:lo + HD] = parts[hh].astype(BF16)
            else:
                slot = (g - 1) * HEADS + hh
                scr_ref[slot] = parts[hh]
                for r in range(d):
                    a_ref[c, r, sub_rows, lo:lo + HD] = scr_ref[
                        slot, pl.ds(r, SUB_IN // d, stride=d), :].astype(BF16)

    def gate_tile(t):
        gt_ref[rows, t * TILE:(t + 1) * TILE] = jax.nn.sigmoid(
            _proj(h_ref, w_ref, rows, T_GATE + t)).astype(BF16)

    thunks = [lambda g=g, c=c: attention_tile(g, c) for g in range(N_DIL) for c in range(3)]
    return thunks + [lambda t=t: gate_tile(t) for t in range(N_BRANCH * D_MODEL // TILE)]


def _retention_heads(za_ref, decay_ref, zeta_ref, xi_ref, gch_ref, rg_ref, ya_ref, state_ref):
    n_chunks = za_ref.shape[0] // CHUNK

    def head(hh):
        cols = lambda t: slice(t * TILE + hh * HD, t * TILE + (hh + 1) * HD)
        decay = decay_ref[hh]
        zeta = zeta_ref[hh]
        xi = xi_ref[hh]
        gch = gch_ref[hh]
        gain = rg_ref[:, hh * HD:(hh + 1) * HD]
        inners, kvs = [], []
        for c in range(n_chunks):
            rows = slice(c * CHUNK, (c + 1) * CHUNK)
            q = za_ref[rows, cols(T_RET_Q)]
            k = za_ref[rows, cols(T_RET_K)]
            v = za_ref[rows, cols(T_RET_V)]
            scores = lax.dot_general(q, k, (((1,), (1,)), ((), ())), preferred_element_type=F32) * decay
            inners.append(jnp.dot(scores.astype(BF16), v, preferred_element_type=F32))
            vz = (v.astype(F32) * zeta).astype(BF16)
            kvs.append(lax.dot_general(k, vz, (((0,), (0,)), ((), ())), preferred_element_type=F32))
        state = state_ref[hh]
        for c in range(n_chunks):
            rows = slice(c * CHUNK, (c + 1) * CHUNK)
            cross = jnp.dot(za_ref[rows, cols(T_RET_Q)], state.astype(BF16),
                            preferred_element_type=F32) * xi
            y = inners[c] + cross
            y = y * lax.rsqrt(jnp.mean(y * y, axis=-1, keepdims=True) + EPS) * gain
            ya_ref[rows, hh * HD:(hh + 1) * HD] = (
                za_ref[rows, cols(T_RET_G)].astype(F32) * y).astype(BF16)
            state = state * gch + kvs[c]
        state_ref[hh] = state

    return [lambda hh=hh: head(hh) for hh in range(HEADS)]


def _spatial_gating_groups(za_ref, ws_ref, bs_ref, yb_ref):
    n_chunks = za_ref.shape[0] // CHUNK

    def group(g):
        row = lax.broadcasted_iota(jnp.int32, (CHUNK, CHUNK), 0)
        col = lax.broadcasted_iota(jnp.int32, (CHUNK, CHUNK), 1)
        cols = lambda t: slice(t * TILE + g * HD, t * TILE + (g + 1) * HD)
        w = jnp.where(row >= col, ws_ref[g], 0.0).astype(BF16)
        bias = bs_ref[g]
        v_all = jnp.concatenate(
            [za_ref[c * CHUNK:(c + 1) * CHUNK, cols(T_SG_V)] for c in range(n_chunks)], axis=1)
        mixed_all = jnp.dot(w, v_all, preferred_element_type=F32)
        for c in range(n_chunks):
            rows = slice(c * CHUNK, (c + 1) * CHUNK)
            mixed = mixed_all[:, c * HD:(c + 1) * HD] + bias
            yb_ref[rows, g * HD:(g + 1) * HD] = (
                za_ref[rows, cols(T_SG_U)].astype(F32) * mixed).astype(BF16)

    return [lambda g=g: group(g) for g in range(HEADS)]


def _inproj(l, x, gmix, w, cos, sin, lng, lnb, qg, kg, decay, zeta, xi, gch, rg, ws, bs, wgu):
    s = x.shape[0]
    tm = TM_IN
    row = lambda i: (i, 0)
    n_steps = s // tm
    wgu_rows = D_MODEL // n_steps
    assert D_MODEL % n_steps == 0 and wgu_rows % 16 == 0
    table = _resident((HEADS, CHUNK, CHUNK))
    layer_table = _layer((HEADS, CHUNK, CHUNK), l)
    out_shape = [jax.ShapeDtypeStruct((s, BW), BF16)] * 2
    out_specs = [pl.BlockSpec((tm, BW), row)] * 2
    for _, d in DIL_PAIRS:
        out_shape.append(jax.ShapeDtypeStruct((3, d, s // d, BW), BF16))
        out_specs.append(pl.BlockSpec((3, d, tm // d, BW), lambda i: (0, 0, i, 0)))
    out_shape.append(jax.ShapeDtypeStruct((s, N_BRANCH * D_MODEL), BF16))
    out_specs.append(pl.BlockSpec((tm, N_BRANCH * D_MODEL), row))
    out_shape.append(jax.ShapeDtypeStruct((D_MODEL, 2 * D_FF), BF16))
    out_specs.append(pl.BlockSpec((wgu_rows, 2 * D_FF), row))
    return pl.pallas_call(
        _inproj_kernel,
        grid=(n_steps,),
        in_specs=[
            pl.BlockSpec((tm, D_MODEL), row),
            _layer((1, D_MODEL), l),
            _resident((D_MODEL, N_IN)),
            pl.BlockSpec((tm, HD // 2), row),
            pl.BlockSpec((tm, HD // 2), row),
            _layer((1, BW), l),
            _layer((1, BW), l),
            _layer((1, HD), l),
            _layer((1, HD), l),
            table, table, table, table,
            _layer((1, BW), l), layer_table, layer_table,
            pl.BlockSpec((None, wgu_rows, 2 * D_FF), lambda i: (l, i, 0)),
        ],
        out_specs=out_specs,
        out_shape=out_shape,
        scratch_shapes=[pltpu.VMEM((tm, D_MODEL), BF16),
                        pltpu.VMEM((tm, N_MIX_TILES * TILE), BF16),
                        pltpu.VMEM((2 * HEADS, SUB_IN, HD), F32),
                        pltpu.VMEM((HEADS, HD, HD), F32)],
        compiler_params=pltpu.CompilerParams(
            dimension_semantics=("arbitrary",), vmem_limit_bytes=VMEM_LIMIT_V7X),
        name="inproj",
    )(x, gmix, w, cos, sin, lng, lnb, qg, kg, decay, zeta, xi, gch, rg, ws, bs, wgu)


def _attn_kernel(n_cast, bound_ref, q_ref, kp_ref, kc_ref, vp_ref, vc_ref, *refs):
    w32_refs = refs[:n_cast]
    o_ref, lse_ref = refs[n_cast:n_cast + 2]
    w16_refs = refs[n_cast + 2:]
    for w32_ref, w16_ref in zip(w32_refs, w16_refs):
        w16_ref[...] = w32_ref[...].astype(BF16)

    tq = q_ref.shape[0]
    first = pl.program_id(1) == 0
    row = lax.broadcasted_iota(jnp.int32, (CHUNK, 2 * CHUNK), 0)
    col = lax.broadcasted_iota(jnp.int32, (CHUNK, 2 * CHUNK), 1)
    band = (col >= row) & (col <= row + CHUNK)
    band_first = band & (col >= jnp.where(first, CHUNK, 0))
    lane = lax.broadcasted_iota(jnp.int32, (CHUNK, HD), 1)
    nt = (((1,), (1,)), ((), ()))
    bound = bound_ref[...]

    def operands(i, hh):
        rows = slice(i * CHUNK, (i + 1) * CHUNK)
        cols = slice(hh * HD, (hh + 1) * HD)
        if i == 0:
            k = jnp.concatenate([kp_ref[:, cols], kc_ref[rows, cols]], axis=0)
            v = jnp.concatenate([vp_ref[:, cols], vc_ref[rows, cols]], axis=0)
        else:
            keys = slice((i - 1) * CHUNK, (i + 1) * CHUNK)
            k = kc_ref[keys, cols]
            v = vc_ref[keys, cols]
        return rows, cols, q_ref[rows, cols], k, v

    shift = jnp.where(band, -bound, NEG_INF)
    shift_first = jnp.where(band_first, -bound, NEG_INF)
    ones = jnp.ones((2 * CHUNK, HD), BF16)
    trusted = jnp.ones((CHUNK, HD), F32)
    for i in range(tq // CHUNK):
        den_tile = jnp.zeros((CHUNK, HD), F32)
        for hh in range(HEADS):
            rows, cols, q, k, v = operands(i, hh)
            s = lax.dot_general(q, k, nt, preferred_element_type=F32)
            p = jnp.exp2(s + (shift_first if i == 0 else shift)).astype(BF16)
            od = jnp.dot(p, jnp.concatenate([v, ones], axis=1), preferred_element_type=F32)
            den = od[:, HD:]
            o_ref[rows, cols] = (od[:, :HD] / den).astype(BF16)
            den_tile = jnp.where(lane // LSE_LANES == hh, den, den_tile)
        lse_ref[i * CHUNK:(i + 1) * CHUNK, :] = (bound[:, :HD] + jnp.log2(den_tile)) * LN2
        trusted = jnp.minimum(trusted, jnp.where(den_tile >= DEN_TRUST_MIN, 1.0, 0.0))

    @pl.when(jnp.min(trusted) < 0.5)
    def _():
        for i in range(tq // CHUNK):
            lse_tile = jnp.zeros((CHUNK, HD), F32)
            for hh in range(HEADS):
                rows, cols, q, k, v = operands(i, hh)
                s = lax.dot_general(q, k, nt, preferred_element_type=F32)
                s = jnp.where(band_first if i == 0 else band, s, NEG_INF)
                m = jnp.max(s, axis=-1, keepdims=True)
                e = jnp.exp2(s - m)
                den = jnp.sum(e, axis=-1, keepdims=True)
                o = jnp.dot(e.astype(BF16), v, preferred_element_type=F32) / den
                o_ref[rows, cols] = o.astype(BF16)
                lse_tile = jnp.where(lane // LSE_LANES == hh, (m + jnp.log2(den)) * LN2, lse_tile)
            lse_ref[i * CHUNK:(i + 1) * CHUNK, :] = lse_tile


def _attention(l, a, d, bound, casts=()):
    length = a.shape[2]
    tq = TQ_ATT
    assert length % tq == 0
    per = tq // CHUNK
    nb = length // tq
    n_steps = d * nb
    cur = lambda c: pl.BlockSpec((None, None, tq, BW), lambda r, b, c=c: (c, r, b, 0))
    prev = lambda c: pl.BlockSpec((None, None, CHUNK, BW),
                                  lambda r, b, c=c: (c, r, jnp.maximum(b * per - 1, 0), 0))
    cast_in, cast_out, cast_shape = [], [], []
    for w, wl in casts:
        _, rows, cols = w.shape
        rt = rows // n_steps
        assert rows % n_steps == 0 and rt % 16 == 0
        cast_in.append(pl.BlockSpec((None, rt, cols), lambda r, b, wl=wl: (wl, r * nb + b, 0)))
        cast_out.append(pl.BlockSpec((rt, cols), lambda r, b: (r * nb + b, 0)))
        cast_shape.append(jax.ShapeDtypeStruct((rows, cols), BF16))
    return pl.pallas_call(
        functools.partial(_attn_kernel, len(casts)),
        grid=(d, nb),
        in_specs=[_layer((1, 2 * CHUNK), l), cur(0), prev(1), cur(1), prev(2), cur(2)] + cast_in,
        out_specs=[pl.BlockSpec((None, tq, BW), lambda r, b: (r, b, 0)),
                   pl.BlockSpec((None, tq, HD), lambda r, b: (r, b, 0))] + cast_out,
        out_shape=[jax.ShapeDtypeStruct((d, length, BW), BF16),
                   jax.ShapeDtypeStruct((d, length, HD), F32)] + cast_shape,
        compiler_params=pltpu.CompilerParams(
            dimension_semantics=("arbitrary", "arbitrary"), vmem_limit_bytes=VMEM_LIMIT_V7X),
        name=f"attn_d{d}",
    )(bound, a, a, a, a, a, *[w for w, _ in casts])


def _merge_ffn_kernel(x_ref, ya_ref, yb_ref, o0_ref, o1_ref, o2_ref, l0_ref, l1_ref, l2_ref, gt_ref,
                      wb_ref, wo_ref, gffn_ref, wgu_ref, wd_ref, out_ref, oscr_ref, lscr_ref, yc_ref):
    tm = x_ref.shape[0]

    def natural_order(src_ref, cols, scr_ref, slot, d):
        if d == 1:
            return src_ref[0, :, cols].astype(F32)
        for r in range(d):
            scr_ref[slot, pl.ds(r, tm // d, stride=d), :] = src_ref[r, :, cols].astype(F32)
        return scr_ref[slot]

    outs, lses = [], []
    for g, (o_ref, l_ref, (_, d)) in enumerate(zip((o0_ref, o1_ref, o2_ref), (l0_ref, l1_ref, l2_ref),
                                                   DIL_PAIRS)):
        outs.append([natural_order(o_ref, slice(hh * HD, (hh + 1) * HD), oscr_ref, g * HEADS + hh, d)
                     for hh in range(HEADS)])
        lses.append(natural_order(l_ref, slice(0, HD), lscr_ref, g, d))
    m = jnp.maximum(jnp.maximum(lses[0], lses[1]), lses[2])
    es = [jnp.exp(l - m) for l in lses]
    inv = 1.0 / (es[0] + es[1] + es[2])
    for hh in range(HEADS):
        cols = slice(hh * HD, (hh + 1) * HD)
        acc = jnp.zeros((tm, HD), F32)
        for g in range(N_DIL):
            wgt = (es[g] * inv)[:, hh * LSE_LANES:hh * LSE_LANES + 1]
            acc = acc + wgt * outs[g][hh]
        yc_ref[:, cols] = acc.astype(BF16)

    merged = jnp.zeros((tm, D_MODEL), F32)
    for b, y in enumerate((ya_ref[...], yb_ref[...], yc_ref[...])):
        gate = gt_ref[:, b * D_MODEL:(b + 1) * D_MODEL].astype(F32)
        merged = merged + gate * jnp.dot(y, wb_ref[b * BW:(b + 1) * BW, :], preferred_element_type=F32)
    x1 = x_ref[...] + jnp.dot(merged.astype(BF16), wo_ref[...], preferred_element_type=F32)

    ms = jnp.mean(x1 * x1, axis=-1, keepdims=True)
    h = (x1 * lax.rsqrt(ms + EPS) * gffn_ref[...]).astype(BF16)
    gate = jnp.dot(h, wgu_ref[:, :D_FF], preferred_element_type=F32)
    up = jnp.dot(h, wgu_ref[:, D_FF:], preferred_element_type=F32)
    act = (jax.nn.silu(gate) * up).astype(BF16)
    out_ref[...] = x1 + jnp.dot(act, wd_ref[...], preferred_element_type=F32)


def _merge_ffn(l, x, ya, yb, outs, lses, gates, wb, wo, gffn, wgu, wd):
    s = x.shape[0]
    tm = TM_OUT
    row = lambda i: (i, 0)
    sub = lambda i: (0, i, 0)
    o_specs = [pl.BlockSpec((d, tm // d, BW), sub) for _, d in DIL_PAIRS]
    l_specs = [pl.BlockSpec((d, tm // d, HD), sub) for _, d in DIL_PAIRS]
    return pl.pallas_call(
        _merge_ffn_kernel,
        grid=(s // tm,),
        in_specs=[pl.BlockSpec((tm, D_MODEL), row),
                  pl.BlockSpec((tm, BW), row), pl.BlockSpec((tm, BW), row)]
                 + o_specs + l_specs
                 + [pl.BlockSpec((tm, N_BRANCH * D_MODEL), row),
                    _resident((N_BRANCH * BW, D_MODEL)),
                    _resident((D_MODEL, D_MODEL)),
                    _layer((1, D_MODEL), l),
                    _resident((D_MODEL, 2 * D_FF)),
                    _resident((D_FF, D_MODEL))],
        out_specs=pl.BlockSpec((tm, D_MODEL), row),
        out_shape=jax.ShapeDtypeStruct((s, D_MODEL), F32),
        scratch_shapes=[pltpu.VMEM((N_DIL * HEADS, tm, HD), F32), pltpu.VMEM((N_DIL, tm, HD), F32),
                        pltpu.VMEM((tm, BW), BF16)],
        compiler_params=pltpu.CompilerParams(
            dimension_semantics=("arbitrary",), vmem_limit_bytes=VMEM_LIMIT_V7X),
        name="merge_ffn",
    )(x, ya, yb, *outs, *lses, gates, wb, wo, gffn, wgu, wd)


def _retention_tables():
    c = CHUNK
    log_g = jnp.log1p(-(2.0 ** (-5.0 - jnp.arange(HEADS, dtype=F32))))
    n = jnp.arange(c, dtype=F32)
    diff = n[:, None] - n[None, :]
    scale = HD ** -0.5
    decay = jnp.where(diff >= 0, jnp.exp(log_g[:, None, None] * jnp.maximum(diff, 0.0)), 0.0) * scale
    zeta = jnp.exp(log_g[:, None] * (c - 1.0 - n)[None, :]) * scale
    xi = jnp.exp(log_g[:, None] * (n + 1.0)[None, :])
    g_chunk = jnp.exp(log_g * c)
    full = lambda t: jnp.broadcast_to(t, (HEADS, c, c)).astype(F32)
    return decay, full(zeta[:, :, None]), full(xi[:, :, None]), full(g_chunk[:, None, None])


def _rotary_tables(s):
    half = HD // 2
    inv = 1.0 / (ROPE_BASE ** (jnp.arange(half, dtype=F32) / half))
    ang = jnp.arange(s, dtype=jnp.int32).astype(F32)[:, None] * inv[None, :]
    return jnp.cos(ang), jnp.sin(ang)


def kernel(x, g_mix, w_in, ret_norm_g, sg_ln_g, sg_ln_b, sg_w, sg_b, dil_q_norm_g, dil_k_norm_g,
           w_branch, w_o, g_ffn, w_gate_up, w_down):
    batch, s, d_model = x.shape
    depth = w_in.shape[0]
    assert batch == 1 and d_model == D_MODEL and w_in.shape[-1] == N_IN
    assert s % (DIL_PAIRS[-1][1] * CHUNK) == 0 and s % TM_IN == 0 and s % TM_OUT == 0
    for win, d in DIL_PAIRS:
        assert win // d == CHUNK and (s // d) % TQ_ATT == 0
        assert SUB_IN % (16 * d) == 0 and TM_IN % SUB_IN == 0 and TM_OUT % (16 * d) == 0

    w_in_l = w_in[0].astype(BF16)
    w_branch_rows = w_branch.reshape(depth, N_BRANCH * BW, D_MODEL)
    cos, sin = _rotary_tables(s)
    decay, zeta, xi, gch = _retention_tables()
    bias_s = jnp.broadcast_to(sg_b[:, :, :, None], sg_b.shape + (CHUNK,)).astype(F32)
    vec = lambda p: p[:, None, :]
    score_bound = (LOG2E * HD ** 0.5) * (jnp.max(jnp.abs(dil_q_norm_g), axis=-1)
                                         * jnp.max(jnp.abs(dil_k_norm_g), axis=-1))
    score_bound = jnp.broadcast_to(score_bound[:, None, None], (depth, 1, 2 * CHUNK)).astype(F32)

    xs = x.reshape(s, d_model)
    for l in range(depth):
        ya, yb, a0, a1, a2, gates, w_gu_l = _inproj(
            l, xs, vec(g_mix), w_in_l, cos, sin, vec(sg_ln_g), vec(sg_ln_b),
            vec(dil_q_norm_g), vec(dil_k_norm_g), decay, zeta, xi, gch, vec(ret_norm_g), sg_w, bias_s,
            w_gate_up)
        casts = ([(w_down, l)],
                 [(w_branch_rows, l), (w_o, l)],
                 [(w_in, l + 1)] if l + 1 < depth else [])
        outs, lses, w16 = [], [], []
        for a, (_, d), cast in zip((a0, a1, a2), DIL_PAIRS, casts):
            o, lse, *copies = _attention(l, a, d, score_bound, cast)
            outs.append(o)
            lses.append(lse)
            w16 += copies
        w_d_l, w_b_l, w_o_l = w16[:3]
        xs = _merge_ffn(l, xs, ya, yb, outs, lses, gates, w_b_l, w_o_l, vec(g_ffn), w_gu_l, w_d_l)
        if l + 1 < depth:
            w_in_l = w16[3]
    return xs.reshape(batch, s, d_model)
```

```python
import functools

import jax
import jax.numpy as jnp
from jax import lax
from jax.experimental import pallas as pl
from jax.experimental.pallas import tpu as pltpu

F32 = jnp.float32
BF16 = jnp.bfloat16

D_MODEL = 1024
EPS = 1e-6
NEG_INF = -1e30
HEADS = 4
HD = 128
CHUNK = 128
ROPE_BASE = 10000.0
DIL_PAIRS = ((128, 1), (512, 4), (2048, 16))
N_DIL = len(DIL_PAIRS)
BW = HEADS * HD
N_BRANCH = 3
D_FF = 2816
N_IN = 10752
TILE = 512
LSE_LANES = HD // HEADS
LOG2E = 1.4426950408889634
LN2 = 0.6931471805599453
DEN_TRUST_MIN = 1e-30

T_RET_Q, T_RET_K, T_RET_V, T_RET_G, T_SG_U, T_SG_V = range(6)
T_DIL_Q, T_DIL_K, T_DIL_V = 6, 6 + N_DIL, 6 + 2 * N_DIL
T_GATE = 6 + 3 * N_DIL
N_MIX_TILES = 6

VMEM_LIMIT_V7X = 56 * 1024 * 1024

TM_IN = 512
SUB_IN = 256
MIX_EVERY = 2
TQ_ATT = 2048
TM_OUT = 512


def _resident(shape):
    nd = len(shape)
    return pl.BlockSpec(shape, lambda *_: (0,) * nd, pipeline_mode=pl.Buffered(1))


def _layer(shape, l):
    nd = len(shape)
    return pl.BlockSpec((None,) + tuple(shape), lambda *_: (l,) + (0,) * nd,
                        pipeline_mode=pl.Buffered(1))


def _inproj_kernel(x_ref, gmix_ref, w_ref, cos_ref, sin_ref, lng_ref, lnb_ref, qg_ref, kg_ref,
                   decay_ref, zeta_ref, xi_ref, gch_ref, rg_ref, ws_ref, bs_ref, wgu32_ref,
                   ya_ref, yb_ref, a0_ref, a1_ref, a2_ref, gt_ref, wgu16_ref,
                   h_ref, za_ref, scr_ref, state_ref):
    n_sub = x_ref.shape[0] // SUB_IN
    wgu16_ref[...] = wgu32_ref[...].astype(BF16)

    @pl.when(pl.program_id(0) == 0)
    def _():
        state_ref[...] = jnp.zeros_like(state_ref)

    for sb in range(n_sub):
        _project_mixer_inputs(sb, x_ref, gmix_ref, w_ref, cos_ref, sin_ref, lng_ref, lnb_ref,
                              h_ref, za_ref)
    mixers = (_retention_heads(za_ref, decay_ref, zeta_ref, xi_ref, gch_ref, rg_ref, ya_ref, state_ref)
              + _spatial_gating_groups(za_ref, ws_ref, bs_ref, yb_ref))
    tiles = []
    for sb in range(n_sub):
        tiles += _attention_and_gate_tiles(sb, w_ref, qg_ref, kg_ref, h_ref, (a0_ref, a1_ref, a2_ref),
                                           gt_ref, scr_ref)
    for i, tile in enumerate(tiles):
        tile()
        if i % MIX_EVERY == MIX_EVERY - 1 and i // MIX_EVERY < len(mixers):
            mixers[i // MIX_EVERY]()
    assert len(tiles) // MIX_EVERY >= len(mixers)


def _proj(h_ref, w_ref, rows, t):
    return jnp.dot(h_ref[rows, :], w_ref[:, t * TILE:(t + 1) * TILE], preferred_element_type=F32)


def _project_mixer_inputs(sb, x_ref, gmix_ref, w_ref, cos_ref, sin_ref, lng_ref, lnb_ref,
                          h_ref, za_ref):
    rows = slice(sb * SUB_IN, (sb + 1) * SUB_IN)
    x = x_ref[rows, :]
    ms = jnp.mean(x * x, axis=-1, keepdims=True)
    h_ref[rows, :] = (x * lax.rsqrt(ms + EPS) * gmix_ref[...]).astype(BF16)

    c = cos_ref[rows, :]
    s = sin_ref[rows, :]
    cos = jnp.concatenate([c, c], axis=1)
    sin = jnp.concatenate([-s, s], axis=1)
    for t in (T_RET_Q, T_RET_K):
        acc = _proj(h_ref, w_ref, rows, t)
        for hh in range(HEADS):
            a = acc[:, hh * HD:(hh + 1) * HD]
            za_ref[rows, t * TILE + hh * HD:t * TILE + (hh + 1) * HD] = (
                a * cos + pltpu.roll(a, HD // 2, 1) * sin).astype(BF16)
    tile = lambda t: slice(t * TILE, (t + 1) * TILE)
    za_ref[rows, tile(T_RET_V)] = _proj(h_ref, w_ref, rows, T_RET_V).astype(BF16)
    za_ref[rows, tile(T_RET_G)] = jax.nn.silu(_proj(h_ref, w_ref, rows, T_RET_G)).astype(BF16)
    za_ref[rows, tile(T_SG_U)] = jax.nn.gelu(_proj(h_ref, w_ref, rows, T_SG_U)).astype(BF16)
    v = jax.nn.gelu(_proj(h_ref, w_ref, rows, T_SG_V))
    mu = jnp.mean(v, axis=-1, keepdims=True)
    vc = v - mu
    var = jnp.mean(vc * vc, axis=-1, keepdims=True)
    za_ref[rows, tile(T_SG_V)] = (vc * lax.rsqrt(var + EPS) * lng_ref[...] + lnb_ref[...]).astype(BF16)


def _attention_and_gate_tiles(sb, w_ref, qg_ref, kg_ref, h_ref, a_refs, gt_ref, scr_ref):
    rows = slice(sb * SUB_IN, (sb + 1) * SUB_IN)

    def head_norm(acc, gain):
        outs = []
        for hh in range(HEADS):
            a = acc[:, hh * HD:(hh + 1) * HD]
            outs.append(a * lax.rsqrt(jnp.mean(a * a, axis=-1, keepdims=True) + EPS) * gain)
        return outs

    def attention_tile(g, c):
        a_ref, (_, d) = a_refs[g], DIL_PAIRS[g]
        sub_rows = slice(sb * SUB_IN // d, (sb + 1) * SUB_IN // d)
        acc = _proj(h_ref, w_ref, rows, (T_DIL_Q, T_DIL_K, T_DIL_V)[c] + g)
        if c == 0:
            parts = head_norm(acc, qg_ref[...] * (LOG2E * HD ** -0.5))
        elif c == 1:
            parts = head_norm(acc, kg_ref[...])
        else:
            parts = [acc[:, hh * HD:(hh + 1) * HD] for hh in range(HEADS)]
        for hh in range(HEADS):
            lo = hh * HD
            if d == 1:
                a_ref[c, 0, sub_rows, lo:lo + HD] = parts[hh].astype(BF16)
            else:
                slot = (g - 1) * HEADS + hh
                scr_ref[slot] = parts[hh]
                for r in range(d):
                    a_ref[c, r, sub_rows, lo:lo + HD] = scr_ref[
                        slot, pl.ds(r, SUB_IN // d, stride=d), :].astype(BF16)

    def gate_tile(t):
        gt_ref[rows, t * TILE:(t + 1) * TILE] = jax.nn.sigmoid(
            _proj(h_ref, w_ref, rows, T_GATE + t)).astype(BF16)

    thunks = [lambda g=g, c=c: attention_tile(g, c) for g in range(N_DIL) for c in range(3)]
    return thunks + [lambda t=t: gate_tile(t) for t in range(N_BRANCH * D_MODEL // TILE)]


def _retention_heads(za_ref, decay_ref, zeta_ref, xi_ref, gch_ref, rg_ref, ya_ref, state_ref):
    n_chunks = za_ref.shape[0] // CHUNK

    def head(hh):
        cols = lambda t: slice(t * TILE + hh * HD, t * TILE + (hh + 1) * HD)
        decay = decay_ref[hh]
        zeta = zeta_ref[hh]
        xi = xi_ref[hh]
        gch = gch_ref[hh]
        gain = rg_ref[:, hh * HD:(hh + 1) * HD]
        inners, kvs = [], []
        for c in range(n_chunks):
            rows = slice(c * CHUNK, (c + 1) * CHUNK)
            q = za_ref[rows, cols(T_RET_Q)]
            k = za_ref[rows, cols(T_RET_K)]
            v = za_ref[rows, cols(T_RET_V)]
            scores = lax.dot_general(q, k, (((1,), (1,)), ((), ())), preferred_element_type=F32) * decay
            inners.append(jnp.dot(scores.astype(BF16), v, preferred_element_type=F32))
            vz = (v.astype(F32) * zeta).astype(BF16)
            kvs.append(lax.dot_general(k, vz, (((0,), (0,)), ((), ())), preferred_element_type=F32))
        state = state_ref[hh]
        for c in range(n_chunks):
            rows = slice(c * CHUNK, (c + 1) * CHUNK)
            cross = jnp.dot(za_ref[rows, cols(T_RET_Q)], state.astype(BF16),
                            preferred_element_type=F32) * xi
            y = inners[c] + cross
            y = y * lax.rsqrt(jnp.mean(y * y, axis=-1, keepdims=True) + EPS) * gain
            ya_ref[rows, hh * HD:(hh + 1) * HD] = (
                za_ref[rows, cols(T_RET_G)].astype(F32) * y).astype(BF16)
            state = state * gch + kvs[c]
        state_ref[hh] = state

    return [lambda hh=hh: head(hh) for hh in range(HEADS)]


def _spatial_gating_groups(za_ref, ws_ref, bs_ref, yb_ref):
    n_chunks = za_ref.shape[0] // CHUNK

    def group(g):
        row = lax.broadcasted_iota(jnp.int32, (CHUNK, CHUNK), 0)
        col = lax.broadcasted_iota(jnp.int32, (CHUNK, CHUNK), 1)
        cols = lambda t: slice(t * TILE + g * HD, t * TILE + (g + 1) * HD)
        w = jnp.where(row >= col, ws_ref[g], 0.0).astype(BF16)
        bias = bs_ref[g]
        v_all = jnp.concatenate(
            [za_ref[c * CHUNK:(c + 1) * CHUNK, cols(T_SG_V)] for c in range(n_chunks)], axis=1)
        mixed_all = jnp.dot(w, v_all, preferred_element_type=F32)
        for c in range(n_chunks):
            rows = slice(c * CHUNK, (c + 1) * CHUNK)
            mixed = mixed_all[:, c * HD:(c + 1) * HD] + bias
            yb_ref[rows, g * HD:(g + 1) * HD] = (
                za_ref[rows, cols(T_SG_U)].astype(F32) * mixed).astype(BF16)

    return [lambda g=g: group(g) for g in range(HEADS)]


def _inproj(l, x, gmix, w, cos, sin, lng, lnb, qg, kg, decay, zeta, xi, gch, rg, ws, bs, wgu):
    s = x.shape[0]
    tm = TM_IN
    row = lambda i: (i, 0)
    n_steps = s // tm
    wgu_rows = D_MODEL // n_steps
    assert D_MODEL % n_steps == 0 and wgu_rows % 16 == 0
    table = _resident((HEADS, CHUNK, CHUNK))
    layer_table = _layer((HEADS, CHUNK, CHUNK), l)
    out_shape = [jax.ShapeDtypeStruct((s, BW), BF16)] * 2
    out_specs = [pl.BlockSpec((tm, BW), row)] * 2
    for _, d in DIL_PAIRS:
        out_shape.append(jax.ShapeDtypeStruct((3, d, s // d, BW), BF16))
        out_specs.append(pl.BlockSpec((3, d, tm // d, BW), lambda i: (0, 0, i, 0)))
    out_shape.append(jax.ShapeDtypeStruct((s, N_BRANCH * D_MODEL), BF16))
    out_specs.append(pl.BlockSpec((tm, N_BRANCH * D_MODEL), row))
    out_shape.append(jax.ShapeDtypeStruct((D_MODEL, 2 * D_FF), BF16))
    out_specs.append(pl.BlockSpec((wgu_rows, 2 * D_FF), row))
    return pl.pallas_call(
        _inproj_kernel,
        grid=(n_steps,),
        in_specs=[
            pl.BlockSpec((tm, D_MODEL), row),
            _layer((1, D_MODEL), l),
            _resident((D_MODEL, N_IN)),
            pl.BlockSpec((tm, HD // 2), row),
            pl.BlockSpec((tm, HD // 2), row),
            _layer((1, BW), l),
            _layer((1, BW), l),
            _layer((1, HD), l),
            _layer((1, HD), l),
            table, table, table, table,
            _layer((1, BW), l), layer_table, layer_table,
            pl.BlockSpec((None, wgu_rows, 2 * D_FF), lambda i: (l, i, 0)),
        ],
        out_specs=out_specs,
        out_shape=out_shape,
        scratch_shapes=[pltpu.VMEM((tm, D_MODEL), BF16),
                        pltpu.VMEM((tm, N_MIX_TILES * TILE), BF16),
                        pltpu.VMEM((2 * HEADS, SUB_IN, HD), F32),
                        pltpu.VMEM((HEADS, HD, HD), F32)],
        compiler_params=pltpu.CompilerParams(
            dimension_semantics=("arbitrary",), vmem_limit_bytes=VMEM_LIMIT_V7X),
        name="inproj",
    )(x, gmix, w, cos, sin, lng, lnb, qg, kg, decay, zeta, xi, gch, rg, ws, bs, wgu)


def _attn_kernel(n_cast, bound_ref, q_ref, kp_ref, kc_ref, vp_ref, vc_ref, *refs):
    w32_refs = refs[:n_cast]
    o_ref, lse_ref = refs[n_cast:n_cast + 2]
    w16_refs = refs[n_cast + 2:]
    for w32_ref, w16_ref in zip(w32_refs, w16_refs):
        w16_ref[...] = w32_ref[...].astype(BF16)

    tq = q_ref.shape[0]
    first = pl.program_id(1) == 0
    row = lax.broadcasted_iota(jnp.int32, (CHUNK, 2 * CHUNK), 0)
    col = lax.broadcasted_iota(jnp.int32, (CHUNK, 2 * CHUNK), 1)
    band = (col >= row) & (col <= row + CHUNK)
    band_first = band & (col >= jnp.where(first, CHUNK, 0))
    lane = lax.broadcasted_iota(jnp.int32, (CHUNK, HD), 1)
    nt = (((1,), (1,)), ((), ()))
    bound = bound_ref[...]

    def operands(i, hh):
        rows = slice(i * CHUNK, (i + 1) * CHUNK)
        cols = slice(hh * HD, (hh + 1) * HD)
        if i == 0:
            k = jnp.concatenate([kp_ref[:, cols], kc_ref[rows, cols]], axis=0)
            v = jnp.concatenate([vp_ref[:, cols], vc_ref[rows, cols]], axis=0)
        else:
            keys = slice((i - 1) * CHUNK, (i + 1) * CHUNK)
            k = kc_ref[keys, cols]
            v = vc_ref[keys, cols]
        return rows, cols, q_ref[rows, cols], k, v

    shift = jnp.where(band, -bound, NEG_INF)
    shift_first = jnp.where(band_first, -bound, NEG_INF)
    ones = jnp.ones((2 * CHUNK, HD), BF16)
    trusted = jnp.ones((CHUNK, HD), F32)
    for i in range(tq // CHUNK):
        den_tile = jnp.zeros((CHUNK, HD), F32)
        for hh in range(HEADS):
            rows, cols, q, k, v = operands(i, hh)
            s = lax.dot_general(q, k, nt, preferred_element_type=F32)
            p = jnp.exp2(s + (shift_first if i == 0 else shift)).astype(BF16)
            od = jnp.dot(p, jnp.concatenate([v, ones], axis=1), preferred_element_type=F32)
            den = od[:, HD:]
            o_ref[rows, cols] = (od[:, :HD] / den).astype(BF16)
            den_tile = jnp.where(lane // LSE_LANES == hh, den, den_tile)
        lse_ref[i * CHUNK:(i + 1) * CHUNK, :] = (bound[:, :HD] + jnp.log2(den_tile)) * LN2
        trusted = jnp.minimum(trusted, jnp.where(den_tile >= DEN_TRUST_MIN, 1.0, 0.0))

    @pl.when(jnp.min(trusted) < 0.5)
    def _():
        for i in range(tq // CHUNK):
            lse_tile = jnp.zeros((CHUNK, HD), F32)
            for hh in range(HEADS):
                rows, cols, q, k, v = operands(i, hh)
                s = lax.dot_general(q, k, nt, preferred_element_type=F32)
                s = jnp.where(band_first if i == 0 else band, s, NEG_INF)
                m = jnp.max(s, axis=-1, keepdims=True)
                e = jnp.exp2(s - m)
                den = jnp.sum(e, axis=-1, keepdims=True)
                o = jnp.dot(e.astype(BF16), v, preferred_element_type=F32) / den
                o_ref[rows, cols] = o.astype(BF16)
                lse_tile = jnp.where(lane // LSE_LANES == hh, (m + jnp.log2(den)) * LN2, lse_tile)
            lse_ref[i * CHUNK:(i + 1) * CHUNK, :] = lse_tile


def _attention(l, a, d, bound, casts=()):
    length = a.shape[2]
    tq = min(TQ_ATT, length)
    assert length % tq == 0
    per = tq // CHUNK
    nb = length // tq
    n_steps = d * nb
    cur = lambda c: pl.BlockSpec((None, None, tq, BW), lambda r, b, c=c: (c, r, b, 0))
    prev = lambda c: pl.BlockSpec((None, None, CHUNK, BW),
                                  lambda r, b, c=c: (c, r, jnp.maximum(b * per - 1, 0), 0))
    cast_in, cast_out, cast_shape = [], [], []
    for w, wl in casts:
        _, rows, cols = w.shape
        rt = rows // n_steps
        assert rows % n_steps == 0 and rt % 16 == 0
        cast_in.append(pl.BlockSpec((None, rt, cols), lambda r, b, wl=wl: (wl, r * nb + b, 0)))
        cast_out.append(pl.BlockSpec((rt, cols), lambda r, b: (r * nb + b, 0)))
        cast_shape.append(jax.ShapeDtypeStruct((rows, cols), BF16))
    return pl.pallas_call(
        functools.partial(_attn_kernel, len(casts)),
        grid=(d, nb),
        in_specs=[_layer((1, 2 * CHUNK), l), cur(0), prev(1), cur(1), prev(2), cur(2)] + cast_in,
        out_specs=[pl.BlockSpec((None, tq, BW), lambda r, b: (r, b, 0)),
                   pl.BlockSpec((None, tq, HD), lambda r, b: (r, b, 0))] + cast_out,
        out_shape=[jax.ShapeDtypeStruct((d, length, BW), BF16),
                   jax.ShapeDtypeStruct((d, length, HD), F32)] + cast_shape,
        compiler_params=pltpu.CompilerParams(
            dimension_semantics=("arbitrary", "arbitrary"), vmem_limit_bytes=VMEM_LIMIT_V7X),
        name=f"attn_d{d}",
    )(bound, a, a, a, a, a, *[w for w, _ in casts])


def _merge_ffn_kernel(x_ref, ya_ref, yb_ref, o0_ref, o1_ref, o2_ref, l0_ref, l1_ref, l2_ref, gt_ref,
                      wb_ref, wo_ref, gffn_ref, wgu_ref, wd_ref, out_ref, oscr_ref, lscr_ref, yc_ref):
    tm = x_ref.shape[0]

    def natural_order(src_ref, cols, scr_ref, slot, d):
        if d == 1:
            return src_ref[0, :, cols].astype(F32)
        for r in range(d):
            scr_ref[slot, pl.ds(r, tm // d, stride=d), :] = src_ref[r, :, cols].astype(F32)
        return scr_ref[slot]

    outs, lses = [], []
    for g, (o_ref, l_ref, (_, d)) in enumerate(zip((o0_ref, o1_ref, o2_ref), (l0_ref, l1_ref, l2_ref),
                                                   DIL_PAIRS)):
        outs.append([natural_order(o_ref, slice(hh * HD, (hh + 1) * HD), oscr_ref, g * HEADS + hh, d)
                     for hh in range(HEADS)])
        lses.append(natural_order(l_ref, slice(0, HD), lscr_ref, g, d))
    m = jnp.maximum(jnp.maximum(lses[0], lses[1]), lses[2])
    es = [jnp.exp(l - m) for l in lses]
    inv = 1.0 / (es[0] + es[1] + es[2])
    for hh in range(HEADS):
        cols = slice(hh * HD, (hh + 1) * HD)
        acc = jnp.zeros((tm, HD), F32)
        for g in range(N_DIL):
            wgt = (es[g] * inv)[:, hh * LSE_LANES:hh * LSE_LANES + 1]
            acc = acc + wgt * outs[g][hh]
        yc_ref[:, cols] = acc.astype(BF16)

    merged = jnp.zeros((tm, D_MODEL), F32)
    for b, y in enumerate((ya_ref[...], yb_ref[...], yc_ref[...])):
        gate = gt_ref[:, b * D_MODEL:(b + 1) * D_MODEL].astype(F32)
        merged = merged + gate * jnp.dot(y, wb_ref[b * BW:(b + 1) * BW, :], preferred_element_type=F32)
    x1 = x_ref[...] + jnp.dot(merged.astype(BF16), wo_ref[...], preferred_element_type=F32)

    ms = jnp.mean(x1 * x1, axis=-1, keepdims=True)
    h = (x1 * lax.rsqrt(ms + EPS) * gffn_ref[...]).astype(BF16)
    gate = jnp.dot(h, wgu_ref[:, :D_FF], preferred_element_type=F32)
    up = jnp.dot(h, wgu_ref[:, D_FF:], preferred_element_type=F32)
    act = (jax.nn.silu(gate) * up).astype(BF16)
    out_ref[...] = x1 + jnp.dot(act, wd_ref[...], preferred_element_type=F32)


def _merge_ffn(l, x, ya, yb, outs, lses, gates, wb, wo, gffn, wgu, wd):
    s = x.shape[0]
    tm = TM_OUT
    row = lambda i: (i, 0)
    sub = lambda i: (0, i, 0)
    o_specs = [pl.BlockSpec((d, tm // d, BW), sub) for _, d in DIL_PAIRS]
    l_specs = [pl.BlockSpec((d, tm // d, HD), sub) for _, d in DIL_PAIRS]
    return pl.pallas_call(
        _merge_ffn_kernel,
        grid=(s // tm,),
        in_specs=[pl.BlockSpec((tm, D_MODEL), row),
                  pl.BlockSpec((tm, BW), row), pl.BlockSpec((tm, BW), row)]
                 + o_specs + l_specs
                 + [pl.BlockSpec((tm, N_BRANCH * D_MODEL), row),
                    _resident((N_BRANCH * BW, D_MODEL)),
                    _resident((D_MODEL, D_MODEL)),
                    _layer((1, D_MODEL), l),
                    _resident((D_MODEL, 2 * D_FF)),
                    _resident((D_FF, D_MODEL))],
        out_specs=pl.BlockSpec((tm, D_MODEL), row),
        out_shape=jax.ShapeDtypeStruct((s, D_MODEL), F32),
        scratch_shapes=[pltpu.VMEM((N_DIL * HEADS, tm, HD), F32), pltpu.VMEM((N_DIL, tm, HD), F32),
                        pltpu.VMEM((tm, BW), BF16)],
        compiler_params=pltpu.CompilerParams(
            dimension_semantics=("arbitrary",), vmem_limit_bytes=VMEM_LIMIT_V7X),
        name="merge_ffn",
    )(x, ya, yb, *outs, *lses, gates, wb, wo, gffn, wgu, wd)


def _retention_tables():
    c = CHUNK
    log_g = jnp.log1p(-(2.0 ** (-5.0 - jnp.arange(HEADS, dtype=F32))))
    n = jnp.arange(c, dtype=F32)
    diff = n[:, None] - n[None, :]
    scale = HD ** -0.5
    decay = jnp.where(diff >= 0, jnp.exp(log_g[:, None, None] * jnp.maximum(diff, 0.0)), 0.0) * scale
    zeta = jnp.exp(log_g[:, None] * (c - 1.0 - n)[None, :]) * scale
    xi = jnp.exp(log_g[:, None] * (n + 1.0)[None, :])
    g_chunk = jnp.exp(log_g * c)
    full = lambda t: jnp.broadcast_to(t, (HEADS, c, c)).astype(F32)
    return decay, full(zeta[:, :, None]), full(xi[:, :, None]), full(g_chunk[:, None, None])


def _rotary_tables(s):
    half = HD // 2
    inv = 1.0 / (ROPE_BASE ** (jnp.arange(half, dtype=F32) / half))
    ang = jnp.arange(s, dtype=jnp.int32).astype(F32)[:, None] * inv[None, :]
    return jnp.cos(ang), jnp.sin(ang)


def kernel(x, g_mix, w_in, ret_norm_g, sg_ln_g, sg_ln_b, sg_w, sg_b, dil_q_norm_g, dil_k_norm_g,
           w_branch, w_o, g_ffn, w_gate_up, w_down):
    batch, s, d_model = x.shape
    depth = w_in.shape[0]
    assert batch == 1 and d_model == D_MODEL and w_in.shape[-1] == N_IN
    assert s % (DIL_PAIRS[-1][1] * CHUNK) == 0 and s % TM_IN == 0 and s % TM_OUT == 0
    for win, d in DIL_PAIRS:
        assert win // d == CHUNK and (s // d) % min(TQ_ATT, s // d) == 0
        assert SUB_IN % (16 * d) == 0 and TM_IN % SUB_IN == 0 and TM_OUT % (16 * d) == 0

    w_in_l = w_in[0].astype(BF16)
    w_branch_rows = w_branch.reshape(depth, N_BRANCH * BW, D_MODEL)
    cos, sin = _rotary_tables(s)
    decay, zeta, xi, gch = _retention_tables()
    bias_s = jnp.broadcast_to(sg_b[:, :, :, None], sg_b.shape + (CHUNK,)).astype(F32)
    vec = lambda p: p[:, None, :]
    score_bound = (LOG2E * HD ** 0.5) * (jnp.max(jnp.abs(dil_q_norm_g), axis=-1)
                                         * jnp.max(jnp.abs(dil_k_norm_g), axis=-1))
    score_bound = jnp.broadcast_to(score_bound[:, None, None], (depth, 1, 2 * CHUNK)).astype(F32)

    xs = x.reshape(s, d_model)
    for l in range(depth):
        ya, yb, a0, a1, a2, gates, w_gu_l = _inproj(
            l, xs, vec(g_mix), w_in_l, cos, sin, vec(sg_ln_g), vec(sg_ln_b),
            vec(dil_q_norm_g), vec(dil_k_norm_g), decay, zeta, xi, gch, vec(ret_norm_g), sg_w, bias_s,
            w_gate_up)
        casts = ([(w_down, l)],
                 [(w_branch_rows, l), (w_o, l)],
                 [(w_in, l + 1)] if l + 1 < depth else [])
        outs, lses, w16 = [], [], []
        for a, (_, d), cast in zip((a0, a1, a2), DIL_PAIRS, casts):
            o, lse, *copies = _attention(l, a, d, score_bound, cast)
            outs.append(o)
            lses.append(lse)
            w16 += copies
        w_d_l, w_b_l, w_o_l = w16[:3]
        xs = _merge_ffn(l, xs, ya, yb, outs, lses, gates, w_b_l, w_o_l, vec(g_ffn), w_gu_l, w_d_l)
        if l + 1 < depth:
            w_in_l = w16[3]
    return xs.reshape(batch, s, d_model)
```

```python
import functools

import jax
import jax.numpy as jnp
from jax import lax
from jax.experimental import pallas as pl
from jax.experimental.pallas import tpu as pltpu

F32 = jnp.float32
BF16 = jnp.bfloat16

D_MODEL = 1024
EPS = 1e-6
NEG_INF = -1e30
HEADS = 4
HD = 128
CHUNK = 128
ROPE_BASE = 10000.0
DIL_PAIRS = ((128, 1), (512, 4), (2048, 16))
N_DIL = len(DIL_PAIRS)
BW = HEADS * HD
N_BRANCH = 3
D_FF = 2816
N_IN = 10752
TILE = 512
LSE_LANES = HD // HEADS
LOG2E = 1.4426950408889634
LN2 = 0.6931471805599453
DEN_TRUST_MIN = 1e-30

T_RET_Q, T_RET_K, T_RET_V, T_RET_G, T_SG_U, T_SG_V = range(6)
T_DIL_Q, T_DIL_K, T_DIL_V = 6, 6 + N_DIL, 6 + 2 * N_DIL
T_GATE = 6 + 3 * N_DIL
N_MIX_TILES = 6

VMEM_LIMIT_V7X = 56 * 1024 * 1024

TM_IN = 512
SUB_IN = 256
MIX_EVERY = 2
TQ_ATT = 2048
TM_OUT = 512


def _resident(shape):
    nd = len(shape)
    return pl.BlockSpec(shape, lambda *_: (0,) * nd, pipeline_mode=pl.Buffered(1))


def _layer(shape, l):
    nd = len(shape)
    return pl.BlockSpec((None,) + tuple(shape), lambda *_: (l,) + (0,) * nd,
                        pipeline_mode=pl.Buffered(1))


def _inproj_kernel(x_ref, gmix_ref, w_ref, cos_ref, sin_ref, lng_ref, lnb_ref, qg_ref, kg_ref,
                   decay_ref, zeta_ref, xi_ref, gch_ref, rg_ref, ws_ref, bs_ref, wgu32_ref,
                   ya_ref, yb_ref, a0_ref, a1_ref, a2_ref, gt_ref, wgu16_ref,
                   h_ref, za_ref, scr_ref, state_ref):
    n_sub = x_ref.shape[0] // SUB_IN
    wgu16_ref[...] = wgu32_ref[...].astype(BF16)

    @pl.when(pl.program_id(0) == 0)
    def _():
        state_ref[...] = jnp.zeros_like(state_ref)

    for sb in range(n_sub):
        _project_mixer_inputs(sb, x_ref, gmix_ref, w_ref, cos_ref, sin_ref, lng_ref, lnb_ref,
                              h_ref, za_ref)
    mixers = (_retention_heads(za_ref, decay_ref, zeta_ref, xi_ref, gch_ref, rg_ref, ya_ref, state_ref)
              + _spatial_gating_groups(za_ref, ws_ref, bs_ref, yb_ref))
    tiles = []
    for sb in range(n_sub):
        tiles += _attention_and_gate_tiles(sb, w_ref, qg_ref, kg_ref, h_ref, (a0_ref, a1_ref, a2_ref),
                                           gt_ref, scr_ref)
    for i, tile in enumerate(tiles):
        tile()
        if i % MIX_EVERY == MIX_EVERY - 1 and i // MIX_EVERY < len(mixers):
            mixers[i // MIX_EVERY]()
    assert len(tiles) // MIX_EVERY >= len(mixers)


def _proj(h_ref, w_ref, rows, t):
    return jnp.dot(h_ref[rows, :], w_ref[:, t * TILE:(t + 1) * TILE], preferred_element_type=F32)


def _project_mixer_inputs(sb, x_ref, gmix_ref, w_ref, cos_ref, sin_ref, lng_ref, lnb_ref,
                          h_ref, za_ref):
    rows = slice(sb * SUB_IN, (sb + 1) * SUB_IN)
    x = x_ref[rows, :]
    ms = jnp.mean(x * x, axis=-1, keepdims=True)
    h_ref[rows, :] = (x * lax.rsqrt(ms + EPS) * gmix_ref[...]).astype(BF16)

    c = cos_ref[rows, :]
    s = sin_ref[rows, :]
    cos = jnp.concatenate([c, c], axis=1)
    sin = jnp.concatenate([-s, s], axis=1)
    for t in (T_RET_Q, T_RET_K):
        acc = _proj(h_ref, w_ref, rows, t)
        for hh in range(HEADS):
            a = acc[:, hh * HD:(hh + 1) * HD]
            za_ref[rows, t * TILE + hh * HD:t * TILE + (hh + 1) * HD] = (
                a * cos + pltpu.roll(a, HD // 2, 1) * sin).astype(BF16)
    tile = lambda t: slice(t * TILE, (t + 1) * TILE)
    za_ref[rows, tile(T_RET_V)] = _proj(h_ref, w_ref, rows, T_RET_V).astype(BF16)
    za_ref[rows, tile(T_RET_G)] = jax.nn.silu(_proj(h_ref, w_ref, rows, T_RET_G)).astype(BF16)
    za_ref[rows, tile(T_SG_U)] = jax.nn.gelu(_proj(h_ref, w_ref, rows, T_SG_U)).astype(BF16)
    v = jax.nn.gelu(_proj(h_ref, w_ref, rows, T_SG_V))
    mu = jnp.mean(v, axis=-1, keepdims=True)
    vc = v - mu
    var = jnp.mean(vc * vc, axis=-1, keepdims=True)
    za_ref[rows, tile(T_SG_V)] = (vc * lax.rsqrt(var + EPS) * lng_ref[...] + lnb_ref[...]).astype(BF16)


def _attention_and_gate_tiles(sb, w_ref, qg_ref, kg_ref, h_ref, a_refs, gt_ref, scr_ref):
    rows = slice(sb * SUB_IN, (sb + 1) * SUB_IN)

    def head_norm(acc, gain):
        outs = []
        for hh in range(HEADS):
            a = acc[:, hh * HD:(hh + 1) * HD]
            outs.append(a * lax.rsqrt(jnp.mean(a * a, axis=-1, keepdims=True) + EPS) * gain)
        return outs

    def attention_tile(g, c):
        a_ref, (_, d) = a_refs[g], DIL_PAIRS[g]
        sub_rows = slice(sb * SUB_IN // d, (sb + 1) * SUB_IN // d)
        acc = _proj(h_ref, w_ref, rows, (T_DIL_Q, T_DIL_K, T_DIL_V)[c] + g)
        if c == 0:
            parts = head_norm(acc, qg_ref[...] * (LOG2E * HD ** -0.5))
        elif c == 1:
            parts = head_norm(acc, kg_ref[...])
        else:
            parts = [acc[:, hh * HD:(hh + 1) * HD] for hh in range(HEADS)]
        for hh in range(HEADS):
            lo = hh * HD
            if d == 1:
                a_ref[c, 0, sub_rows, lo:lo + HD] = parts[hh].astype(BF16)
            else:
                slot = (g - 1) * HEADS + hh
                scr_ref[slot] = parts[hh]
                for r in range(d):
                    a_ref[c, r, sub_rows, lo:lo + HD] = scr_ref[
                        slot, pl.ds(r, SUB_IN // d, stride=d), :].astype(BF16)

    def gate_tile(t):
        gt_ref[rows, t * TILE:(t + 1) * TILE] = jax.nn.sigmoid(
            _proj(h_ref, w_ref, rows, T_GATE + t)).astype(BF16)

    thunks = [lambda g=g, c=c: attention_tile(g, c) for g in range(N_DIL) for c in range(3)]
    return thunks + [lambda t=t: gate_tile(t) for t in range(N_BRANCH * D_MODEL // TILE)]


def _retention_heads(za_ref, decay_ref, zeta_ref, xi_ref, gch_ref, rg_ref, ya_ref, state_ref):
    n_chunks = za_ref.shape[0] // CHUNK

    def head(hh):
        cols = lambda t: slice(t * TILE + hh * HD, t * TILE + (hh + 1) * HD)
        decay = decay_ref[hh]
        zeta = zeta_ref[hh]
        xi = xi_ref[hh]
        gch = gch_ref[hh]
        gain = rg_ref[:, hh * HD:(hh + 1) * HD]
        inners, kvs = [], []
        for c in range(n_chunks):
            rows = slice(c * CHUNK, (c + 1) * CHUNK)
            q = za_ref[rows, cols(T_RET_Q)]
            k = za_ref[rows, cols(T_RET_K)]
            v = za_ref[rows, cols(T_RET_V)]
            scores = lax.dot_general(q, k, (((1,), (1,)), ((), ())), preferred_element_type=F32) * decay
            inners.append(jnp.dot(scores.astype(BF16), v, preferred_element_type=F32))
            vz = (v.astype(F32) * zeta).astype(BF16)
            kvs.append(lax.dot_general(k, vz, (((0,), (0,)), ((), ())), preferred_element_type=F32))
        state = state_ref[hh]
        for c in range(n_chunks):
            rows = slice(c * CHUNK, (c + 1) * CHUNK)
            cross = jnp.dot(za_ref[rows, cols(T_RET_Q)], state.astype(BF16),
                            preferred_element_type=F32) * xi
            y = inners[c] + cross
            y = y * lax.rsqrt(jnp.mean(y * y, axis=-1, keepdims=True) + EPS) * gain
            ya_ref[rows, hh * HD:(hh + 1) * HD] = (
                za_ref[rows, cols(T_RET_G)].astype(F32) * y).astype(BF16)
            state = state * gch + kvs[c]
        state_ref[hh] = state

    return [lambda hh=hh: head(hh) for hh in range(HEADS)]


def _spatial_gating_groups(za_ref, ws_ref, bs_ref, yb_ref):
    n_chunks = za_ref.shape[0] // CHUNK

    def group(g):
        row = lax.broadcasted_iota(jnp.int32, (CHUNK, CHUNK), 0)
        col = lax.broadcasted_iota(jnp.int32, (CHUNK, CHUNK), 1)
        cols = lambda t: slice(t * TILE + g * HD, t * TILE + (g + 1) * HD)
        w = jnp.where(row >= col, ws_ref[g], 0.0).astype(BF16)
        bias = bs_ref[g]
        v_all = jnp.concatenate(
            [za_ref[c * CHUNK:(c + 1) * CHUNK, cols(T_SG_V)] for c in range(n_chunks)], axis=1)
        mixed_all = jnp.dot(w, v_all, preferred_element_type=F32)
        for c in range(n_chunks):
            rows = slice(c * CHUNK, (c + 1) * CHUNK)
            mixed = mixed_all[:, c * HD:(c + 1) * HD] + bias
            yb_ref[rows, g * HD:(g + 1) * HD] = (
                za_ref[rows, cols(T_SG_U)].astype(F32) * mixed).astype(BF16)

    return [lambda g=g: group(g) for g in range(HEADS)]


def _inproj(l, x, gmix, w, cos, sin, lng, lnb, qg, kg, decay, zeta, xi, gch, rg, ws, bs, wgu):
    s = x.shape[0]
    tm = TM_IN
    row = lambda i: (i, 0)
    n_steps = s // tm
    wgu_rows = D_MODEL // n_steps
    assert D_MODEL % n_steps == 0 and wgu_rows % 16 == 0
    table = _resident((HEADS, CHUNK, CHUNK))
    layer_table = _layer((HEADS, CHUNK, CHUNK), l)
    out_shape = [jax.ShapeDtypeStruct((s, BW), BF16)] * 2
    out_specs = [pl.BlockSpec((tm, BW), row)] * 2
    for _, d in DIL_PAIRS:
        out_shape.append(jax.ShapeDtypeStruct((3, d, s // d, BW), BF16))
        out_specs.append(pl.BlockSpec((3, d, tm // d, BW), lambda i: (0, 0, i, 0)))
    out_shape.append(jax.ShapeDtypeStruct((s, N_BRANCH * D_MODEL), BF16))
    out_specs.append(pl.BlockSpec((tm, N_BRANCH * D_MODEL), row))
    out_shape.append(jax.ShapeDtypeStruct((D_MODEL, 2 * D_FF), BF16))
    out_specs.append(pl.BlockSpec((wgu_rows, 2 * D_FF), row))
    return pl.pallas_call(
        _inproj_kernel,
        grid=(n_steps,),
        in_specs=[
            pl.BlockSpec((tm, D_MODEL), row),
            _layer((1, D_MODEL), l),
            _resident((D_MODEL, N_IN)),
            pl.BlockSpec((tm, HD // 2), row),
            pl.BlockSpec((tm, HD // 2), row),
            _layer((1, BW), l),
            _layer((1, BW), l),
            _layer((1, HD), l),
            _layer((1, HD), l),
            table, table, table, table,
            _layer((1, BW), l), layer_table, layer_table,
            pl.BlockSpec((None, wgu_rows, 2 * D_FF), lambda i: (l, i, 0)),
        ],
        out_specs=out_specs,
        out_shape=out_shape,
        scratch_shapes=[pltpu.VMEM((tm, D_MODEL), BF16),
                        pltpu.VMEM((tm, N_MIX_TILES * TILE), BF16),
                        pltpu.VMEM((2 * HEADS, SUB_IN, HD), F32),
                        pltpu.VMEM((HEADS, HD, HD), F32)],
        compiler_params=pltpu.CompilerParams(
            dimension_semantics=("arbitrary",), vmem_limit_bytes=VMEM_LIMIT_V7X),
        name="inproj",
    )(x, gmix, w, cos, sin, lng, lnb, qg, kg, decay, zeta, xi, gch, rg, ws, bs, wgu)


def _attn_kernel(n_cast, bound_ref, q_ref, kp_ref, kc_ref, vp_ref, vc_ref, *refs):
    w32_refs = refs[:n_cast]
    o_ref, lse_ref = refs[n_cast:n_cast + 2]
    w16_refs = refs[n_cast + 2:]
    for w32_ref, w16_ref in zip(w32_refs, w16_refs):
        w16_ref[...] = w32_ref[...].astype(BF16)

    n_res, tq = q_ref.shape[:2]
    first = pl.program_id(1) == 0
    row = lax.broadcasted_iota(jnp.int32, (CHUNK, 2 * CHUNK), 0)
    col = lax.broadcasted_iota(jnp.int32, (CHUNK, 2 * CHUNK), 1)
    band = (col >= row) & (col <= row + CHUNK)
    band_first = band & (col >= jnp.where(first, CHUNK, 0))
    lane = lax.broadcasted_iota(jnp.int32, (CHUNK, HD), 1)
    nt = (((1,), (1,)), ((), ()))
    bound = bound_ref[...]

    def operands(r, i, hh):
        rows = slice(i * CHUNK, (i + 1) * CHUNK)
        cols = slice(hh * HD, (hh + 1) * HD)
        if i == 0:
            k = jnp.concatenate([kp_ref[r, :, cols], kc_ref[r, rows, cols]], axis=0)
            v = jnp.concatenate([vp_ref[r, :, cols], vc_ref[r, rows, cols]], axis=0)
        else:
            keys = slice((i - 1) * CHUNK, (i + 1) * CHUNK)
            k = kc_ref[r, keys, cols]
            v = vc_ref[r, keys, cols]
        return rows, cols, q_ref[r, rows, cols], k, v

    blocks = [(r, i) for r in range(n_res) for i in range(tq // CHUNK)]
    shift = jnp.where(band, -bound, NEG_INF)
    shift_first = jnp.where(band_first, -bound, NEG_INF)
    ones = jnp.ones((2 * CHUNK, HD), BF16)
    trusted = jnp.ones((CHUNK, HD), F32)
    for r, i in blocks:
        den_tile = jnp.zeros((CHUNK, HD), F32)
        for hh in range(HEADS):
            rows, cols, q, k, v = operands(r, i, hh)
            s = lax.dot_general(q, k, nt, preferred_element_type=F32)
            p = jnp.exp2(s + (shift_first if i == 0 else shift)).astype(BF16)
            od = jnp.dot(p, jnp.concatenate([v, ones], axis=1), preferred_element_type=F32)
            den = od[:, HD:]
            o_ref[r, rows, cols] = (od[:, :HD] / den).astype(BF16)
            den_tile = jnp.where(lane // LSE_LANES == hh, den, den_tile)
        lse_ref[r, i * CHUNK:(i + 1) * CHUNK, :] = (bound[:, :HD] + jnp.log2(den_tile)) * LN2
        trusted = jnp.minimum(trusted, jnp.where(den_tile >= DEN_TRUST_MIN, 1.0, 0.0))

    @pl.when(jnp.min(trusted) < 0.5)
    def _():
        for r, i in blocks:
            lse_tile = jnp.zeros((CHUNK, HD), F32)
            for hh in range(HEADS):
                rows, cols, q, k, v = operands(r, i, hh)
                s = lax.dot_general(q, k, nt, preferred_element_type=F32)
                s = jnp.where(band_first if i == 0 else band, s, NEG_INF)
                m = jnp.max(s, axis=-1, keepdims=True)
                e = jnp.exp2(s - m)
                den = jnp.sum(e, axis=-1, keepdims=True)
                o = jnp.dot(e.astype(BF16), v, preferred_element_type=F32) / den
                o_ref[r, rows, cols] = o.astype(BF16)
                lse_tile = jnp.where(lane // LSE_LANES == hh, (m + jnp.log2(den)) * LN2, lse_tile)
            lse_ref[r, i * CHUNK:(i + 1) * CHUNK, :] = lse_tile


def _attention(l, a, d, bound, casts=()):
    length = a.shape[2]
    tq = min(TQ_ATT, length)
    n_res = min(d, TQ_ATT // tq)
    assert length % tq == 0 and d % n_res == 0
    per = tq // CHUNK
    nb = length // tq
    n_steps = d // n_res * nb
    cur = lambda c: pl.BlockSpec((None, n_res, tq, BW), lambda r, b, c=c: (c, r, b, 0))
    prev = lambda c: pl.BlockSpec((None, n_res, CHUNK, BW),
                                  lambda r, b, c=c: (c, r, jnp.maximum(b * per - 1, 0), 0))
    cast_in, cast_out, cast_shape = [], [], []
    for w, wl in casts:
        _, rows, cols = w.shape
        rt = rows // n_steps
        assert rows % n_steps == 0 and rt % 16 == 0
        cast_in.append(pl.BlockSpec((None, rt, cols), lambda r, b, wl=wl: (wl, r * nb + b, 0)))
        cast_out.append(pl.BlockSpec((rt, cols), lambda r, b: (r * nb + b, 0)))
        cast_shape.append(jax.ShapeDtypeStruct((rows, cols), BF16))
    return pl.pallas_call(
        functools.partial(_attn_kernel, len(casts)),
        grid=(d // n_res, nb),
        in_specs=[_layer((1, 2 * CHUNK), l), cur(0), prev(1), cur(1), prev(2), cur(2)] + cast_in,
        out_specs=[pl.BlockSpec((n_res, tq, BW), lambda r, b: (r, b, 0)),
                   pl.BlockSpec((n_res, tq, HD), lambda r, b: (r, b, 0))] + cast_out,
        out_shape=[jax.ShapeDtypeStruct((d, length, BW), BF16),
                   jax.ShapeDtypeStruct((d, length, HD), F32)] + cast_shape,
        compiler_params=pltpu.CompilerParams(
            dimension_semantics=("arbitrary", "arbitrary"), vmem_limit_bytes=VMEM_LIMIT_V7X),
        name=f"attn_d{d}",
    )(bound, a, a, a, a, a, *[w for w, _ in casts])


def _merge_ffn_kernel(x_ref, ya_ref, yb_ref, o0_ref, o1_ref, o2_ref, l0_ref, l1_ref, l2_ref, gt_ref,
                      wb_ref, wo_ref, gffn_ref, wgu_ref, wd_ref, out_ref, oscr_ref, lscr_ref, yc_ref):
    tm = x_ref.shape[0]

    def natural_order(src_ref, cols, scr_ref, slot, d):
        if d == 1:
            return src_ref[0, :, cols].astype(F32)
        for r in range(d):
            scr_ref[slot, pl.ds(r, tm // d, stride=d), :] = src_ref[r, :, cols].astype(F32)
        return scr_ref[slot]

    outs, lses = [], []
    for g, (o_ref, l_ref, (_, d)) in enumerate(zip((o0_ref, o1_ref, o2_ref), (l0_ref, l1_ref, l2_ref),
                                                   DIL_PAIRS)):
        outs.append([natural_order(o_ref, slice(hh * HD, (hh + 1) * HD), oscr_ref, g * HEADS + hh, d)
                     for hh in range(HEADS)])
        lses.append(natural_order(l_ref, slice(0, HD), lscr_ref, g, d))
    m = jnp.maximum(jnp.maximum(lses[0], lses[1]), lses[2])
    es = [jnp.exp(l - m) for l in lses]
    inv = 1.0 / (es[0] + es[1] + es[2])
    for hh in range(HEADS):
        cols = slice(hh * HD, (hh + 1) * HD)
        acc = jnp.zeros((tm, HD), F32)
        for g in range(N_DIL):
            wgt = (es[g] * inv)[:, hh * LSE_LANES:hh * LSE_LANES + 1]
            acc = acc + wgt * outs[g][hh]
        yc_ref[:, cols] = acc.astype(BF16)

    merged = jnp.zeros((tm, D_MODEL), F32)
    for b, y in enumerate((ya_ref[...], yb_ref[...], yc_ref[...])):
        gate = gt_ref[:, b * D_MODEL:(b + 1) * D_MODEL].astype(F32)
        merged = merged + gate * jnp.dot(y, wb_ref[b * BW:(b + 1) * BW, :], preferred_element_type=F32)
    x1 = x_ref[...] + jnp.dot(merged.astype(BF16), wo_ref[...], preferred_element_type=F32)

    ms = jnp.mean(x1 * x1, axis=-1, keepdims=True)
    h = (x1 * lax.rsqrt(ms + EPS) * gffn_ref[...]).astype(BF16)
    gate = jnp.dot(h, wgu_ref[:, :D_FF], preferred_element_type=F32)
    up = jnp.dot(h, wgu_ref[:, D_FF:], preferred_element_type=F32)
    act = (jax.nn.silu(gate) * up).astype(BF16)
    out_ref[...] = x1 + jnp.dot(act, wd_ref[...], preferred_element_type=F32)


def _merge_ffn(l, x, ya, yb, outs, lses, gates, wb, wo, gffn, wgu, wd):
    s = x.shape[0]
    tm = TM_OUT
    row = lambda i: (i, 0)
    sub = lambda i: (0, i, 0)
    o_specs = [pl.BlockSpec((d, tm // d, BW), sub) for _, d in DIL_PAIRS]
    l_specs = [pl.BlockSpec((d, tm // d, HD), sub) for _, d in DIL_PAIRS]
    return pl.pallas_call(
        _merge_ffn_kernel,
        grid=(s // tm,),
        in_specs=[pl.BlockSpec((tm, D_MODEL), row),
                  pl.BlockSpec((tm, BW), row), pl.BlockSpec((tm, BW), row)]
                 + o_specs + l_specs
                 + [pl.BlockSpec((tm, N_BRANCH * D_MODEL), row),
                    _resident((N_BRANCH * BW, D_MODEL)),
                    _resident((D_MODEL, D_MODEL)),
                    _layer((1, D_MODEL), l),
                    _resident((D_MODEL, 2 * D_FF)),
                    _resident((D_FF, D_MODEL))],
        out_specs=pl.BlockSpec((tm, D_MODEL), row),
        out_shape=jax.ShapeDtypeStruct((s, D_MODEL), F32),
        scratch_shapes=[pltpu.VMEM((N_DIL * HEADS, tm, HD), F32), pltpu.VMEM((N_DIL, tm, HD), F32),
                        pltpu.VMEM((tm, BW), BF16)],
        compiler_params=pltpu.CompilerParams(
            dimension_semantics=("arbitrary",), vmem_limit_bytes=VMEM_LIMIT_V7X),
        name="merge_ffn",
    )(x, ya, yb, *outs, *lses, gates, wb, wo, gffn, wgu, wd)


def _retention_tables():
    c = CHUNK
    log_g = jnp.log1p(-(2.0 ** (-5.0 - jnp.arange(HEADS, dtype=F32))))
    n = jnp.arange(c, dtype=F32)
    diff = n[:, None] - n[None, :]
    scale = HD ** -0.5
    decay = jnp.where(diff >= 0, jnp.exp(log_g[:, None, None] * jnp.maximum(diff, 0.0)), 0.0) * scale
    zeta = jnp.exp(log_g[:, None] * (c - 1.0 - n)[None, :]) * scale
    xi = jnp.exp(log_g[:, None] * (n + 1.0)[None, :])
    g_chunk = jnp.exp(log_g * c)
    full = lambda t: jnp.broadcast_to(t, (HEADS, c, c)).astype(F32)
    return decay, full(zeta[:, :, None]), full(xi[:, :, None]), full(g_chunk[:, None, None])


def _rotary_tables(s):
    half = HD // 2
    inv = 1.0 / (ROPE_BASE ** (jnp.arange(half, dtype=F32) / half))
    ang = jnp.arange(s, dtype=jnp.int32).astype(F32)[:, None] * inv[None, :]
    return jnp.cos(ang), jnp.sin(ang)


def kernel(x, g_mix, w_in, ret_norm_g, sg_ln_g, sg_ln_b, sg_w, sg_b, dil_q_norm_g, dil_k_norm_g,
           w_branch, w_o, g_ffn, w_gate_up, w_down):
    batch, s, d_model = x.shape
    depth = w_in.shape[0]
    assert batch == 1 and d_model == D_MODEL and w_in.shape[-1] == N_IN
    assert s % (DIL_PAIRS[-1][1] * CHUNK) == 0 and s % TM_IN == 0 and s % TM_OUT == 0
    for win, d in DIL_PAIRS:
        assert win // d == CHUNK and (s // d) % min(TQ_ATT, s // d) == 0
        assert SUB_IN % (16 * d) == 0 and TM_IN % SUB_IN == 0 and TM_OUT % (16 * d) == 0

    w_in_l = w_in[0].astype(BF16)
    w_branch_rows = w_branch.reshape(depth, N_BRANCH * BW, D_MODEL)
    cos, sin = _rotary_tables(s)
    decay, zeta, xi, gch = _retention_tables()
    bias_s = jnp.broadcast_to(sg_b[:, :, :, None], sg_b.shape + (CHUNK,)).astype(F32)
    vec = lambda p: p[:, None, :]
    score_bound = (LOG2E * HD ** 0.5) * (jnp.max(jnp.abs(dil_q_norm_g), axis=-1)
                                         * jnp.max(jnp.abs(dil_k_norm_g), axis=-1))
    score_bound = jnp.broadcast_to(score_bound[:, None, None], (depth, 1, 2 * CHUNK)).astype(F32)

    xs = x.reshape(s, d_model)
    for l in range(depth):
        ya, yb, a0, a1, a2, gates, w_gu_l = _inproj(
            l, xs, vec(g_mix), w_in_l, cos, sin, vec(sg_ln_g), vec(sg_ln_b),
            vec(dil_q_norm_g), vec(dil_k_norm_g), decay, zeta, xi, gch, vec(ret_norm_g), sg_w, bias_s,
            w_gate_up)
        casts = ([(w_down, l)],
                 [(w_branch_rows, l), (w_o, l)],
                 [(w_in, l + 1)] if l + 1 < depth else [])
        outs, lses, w16 = [], [], []
        for a, (_, d), cast in zip((a0, a1, a2), DIL_PAIRS, casts):
            o, lse, *copies = _attention(l, a, d, score_bound, cast)
            outs.append(o)
            lses.append(lse)
            w16 += copies
        w_d_l, w_b_l, w_o_l = w16[:3]
        xs = _merge_ffn(l, xs, ya, yb, outs, lses, gates, w_b_l, w_o_l, vec(g_ffn), w_gu_l, w_d_l)
        if l + 1 < depth:
            w_in_l = w16[3]
    return xs.reshape(batch, s, d_model)
```

```python
import functools

import jax
import jax.numpy as jnp
from jax import lax
from jax.experimental import pallas as pl
from jax.experimental.pallas import tpu as pltpu

F32 = jnp.float32
BF16 = jnp.bfloat16

D_MODEL = 1024
EPS = 1e-6
NEG_INF = -1e30
HEADS = 4
HD = 128
CHUNK = 128
ROPE_BASE = 10000.0
DIL_PAIRS = ((128, 1), (512, 4), (2048, 16))
N_DIL = len(DIL_PAIRS)
BW = HEADS * HD
N_BRANCH = 3
D_FF = 2816
N_IN = 10752
TILE = 512
LSE_LANES = HD // HEADS
LOG2E = 1.4426950408889634
LN2 = 0.6931471805599453
BOUND_FAST_MAX = 40.0

T_RET_Q, T_RET_K, T_RET_V, T_RET_G, T_SG_U, T_SG_V = range(6)
T_DIL_Q, T_DIL_K, T_DIL_V = 6, 6 + N_DIL, 6 + 2 * N_DIL
T_GATE = 6 + 3 * N_DIL
N_MIX_TILES = 6

VMEM_LIMIT_V7X = 56 * 1024 * 1024

TM_IN = 512
SUB_IN = 256
MIX_EVERY = 2
TQ_ATT = 2048
TM_OUT = 512


def _resident(shape):
    nd = len(shape)
    return pl.BlockSpec(shape, lambda *_: (0,) * nd, pipeline_mode=pl.Buffered(1))


def _layer(shape, l):
    nd = len(shape)
    return pl.BlockSpec((None,) + tuple(shape), lambda *_: (l,) + (0,) * nd,
                        pipeline_mode=pl.Buffered(1))


def _inproj_kernel(x_ref, gmix_ref, w_ref, cos_ref, sin_ref, lng_ref, lnb_ref, qg_ref, kg_ref,
                   decay_ref, zeta_ref, xi_ref, gch_ref, rg_ref, ws_ref, bs_ref, wgu32_ref,
                   ya_ref, yb_ref, a0_ref, a1_ref, a2_ref, gt_ref, wgu16_ref,
                   h_ref, za_ref, scr_ref, state_ref):
    n_sub = x_ref.shape[0] // SUB_IN
    wgu16_ref[...] = wgu32_ref[...].astype(BF16)

    @pl.when(pl.program_id(0) == 0)
    def _():
        state_ref[...] = jnp.zeros_like(state_ref)

    for sb in range(n_sub):
        _project_mixer_inputs(sb, x_ref, gmix_ref, w_ref, cos_ref, sin_ref, lng_ref, lnb_ref,
                              h_ref, za_ref)
    mixers = (_retention_heads(za_ref, decay_ref, zeta_ref, xi_ref, gch_ref, rg_ref, ya_ref, state_ref)
              + _spatial_gating_groups(za_ref, ws_ref, bs_ref, yb_ref))
    tiles = []
    for sb in range(n_sub):
        tiles += _attention_and_gate_tiles(sb, w_ref, qg_ref, kg_ref, h_ref, (a0_ref, a1_ref, a2_ref),
                                           gt_ref, scr_ref)
    for i, tile in enumerate(tiles):
        tile()
        if i % MIX_EVERY == MIX_EVERY - 1 and i // MIX_EVERY < len(mixers):
            mixers[i // MIX_EVERY]()
    assert len(tiles) // MIX_EVERY >= len(mixers)


def _proj(h_ref, w_ref, rows, t):
    return jnp.dot(h_ref[rows, :], w_ref[:, t * TILE:(t + 1) * TILE], preferred_element_type=F32)


def _project_mixer_inputs(sb, x_ref, gmix_ref, w_ref, cos_ref, sin_ref, lng_ref, lnb_ref,
                          h_ref, za_ref):
    rows = slice(sb * SUB_IN, (sb + 1) * SUB_IN)
    x = x_ref[rows, :]
    ms = jnp.mean(x * x, axis=-1, keepdims=True)
    h_ref[rows, :] = (x * lax.rsqrt(ms + EPS) * gmix_ref[...]).astype(BF16)

    c = cos_ref[rows, :]
    s = sin_ref[rows, :]
    cos = jnp.concatenate([c, c], axis=1)
    sin = jnp.concatenate([-s, s], axis=1)
    for t in (T_RET_Q, T_RET_K):
        acc = _proj(h_ref, w_ref, rows, t)
        for hh in range(HEADS):
            a = acc[:, hh * HD:(hh + 1) * HD]
            za_ref[rows, t * TILE + hh * HD:t * TILE + (hh + 1) * HD] = (
                a * cos + pltpu.roll(a, HD // 2, 1) * sin).astype(BF16)
    tile = lambda t: slice(t * TILE, (t + 1) * TILE)
    za_ref[rows, tile(T_RET_V)] = _proj(h_ref, w_ref, rows, T_RET_V).astype(BF16)
    za_ref[rows, tile(T_RET_G)] = jax.nn.silu(_proj(h_ref, w_ref, rows, T_RET_G)).astype(BF16)
    za_ref[rows, tile(T_SG_U)] = jax.nn.gelu(_proj(h_ref, w_ref, rows, T_SG_U)).astype(BF16)
    v = jax.nn.gelu(_proj(h_ref, w_ref, rows, T_SG_V))
    mu = jnp.mean(v, axis=-1, keepdims=True)
    vc = v - mu
    var = jnp.mean(vc * vc, axis=-1, keepdims=True)
    za_ref[rows, tile(T_SG_V)] = (vc * lax.rsqrt(var + EPS) * lng_ref[...] + lnb_ref[...]).astype(BF16)


def _attention_and_gate_tiles(sb, w_ref, qg_ref, kg_ref, h_ref, a_refs, gt_ref, scr_ref):
    rows = slice(sb * SUB_IN, (sb + 1) * SUB_IN)

    def head_norm(acc, gain):
        outs = []
        for hh in range(HEADS):
            a = acc[:, hh * HD:(hh + 1) * HD]
            outs.append(a * lax.rsqrt(jnp.mean(a * a, axis=-1, keepdims=True) + EPS) * gain)
        return outs

    def attention_tile(g, c):
        a_ref, (_, d) = a_refs[g], DIL_PAIRS[g]
        sub_rows = slice(sb * SUB_IN // d, (sb + 1) * SUB_IN // d)
        acc = _proj(h_ref, w_ref, rows, (T_DIL_Q, T_DIL_K, T_DIL_V)[c] + g)
        if c == 0:
            parts = head_norm(acc, qg_ref[...] * (LOG2E * HD ** -0.5))
        elif c == 1:
            parts = head_norm(acc, kg_ref[...])
        else:
            parts = [acc[:, hh * HD:(hh + 1) * HD] for hh in range(HEADS)]
        for hh in range(HEADS):
            lo = hh * HD
            if d == 1:
                a_ref[c, 0, sub_rows, lo:lo + HD] = parts[hh].astype(BF16)
            else:
                slot = (g - 1) * HEADS + hh
                scr_ref[slot] = parts[hh]
                for r in range(d):
                    a_ref[c, r, sub_rows, lo:lo + HD] = scr_ref[
                        slot, pl.ds(r, SUB_IN // d, stride=d), :].astype(BF16)

    def gate_tile(t):
        gt_ref[rows, t * TILE:(t + 1) * TILE] = jax.nn.sigmoid(
            _proj(h_ref, w_ref, rows, T_GATE + t)).astype(BF16)

    thunks = [lambda g=g, c=c: attention_tile(g, c) for g in range(N_DIL) for c in range(3)]
    return thunks + [lambda t=t: gate_tile(t) for t in range(N_BRANCH * D_MODEL // TILE)]


def _retention_heads(za_ref, decay_ref, zeta_ref, xi_ref, gch_ref, rg_ref, ya_ref, state_ref):
    n_chunks = za_ref.shape[0] // CHUNK

    def head(hh):
        cols = lambda t: slice(t * TILE + hh * HD, t * TILE + (hh + 1) * HD)
        decay = decay_ref[hh]
        zeta = zeta_ref[hh]
        xi = xi_ref[hh]
        gch = gch_ref[hh]
        gain = rg_ref[:, hh * HD:(hh + 1) * HD]
        inners, kvs = [], []
        for c in range(n_chunks):
            rows = slice(c * CHUNK, (c + 1) * CHUNK)
            q = za_ref[rows, cols(T_RET_Q)]
            k = za_ref[rows, cols(T_RET_K)]
            v = za_ref[rows, cols(T_RET_V)]
            scores = lax.dot_general(q, k, (((1,), (1,)), ((), ())), preferred_element_type=F32) * decay
            inners.append(jnp.dot(scores.astype(BF16), v, preferred_element_type=F32))
            vz = (v.astype(F32) * zeta).astype(BF16)
            kvs.append(lax.dot_general(k, vz, (((0,), (0,)), ((), ())), preferred_element_type=F32))
        state = state_ref[hh]
        for c in range(n_chunks):
            rows = slice(c * CHUNK, (c + 1) * CHUNK)
            cross = jnp.dot(za_ref[rows, cols(T_RET_Q)], state.astype(BF16),
                            preferred_element_type=F32) * xi
            y = inners[c] + cross
            y = y * lax.rsqrt(jnp.mean(y * y, axis=-1, keepdims=True) + EPS) * gain
            ya_ref[rows, hh * HD:(hh + 1) * HD] = (
                za_ref[rows, cols(T_RET_G)].astype(F32) * y).astype(BF16)
            state = state * gch + kvs[c]
        state_ref[hh] = state

    return [lambda hh=hh: head(hh) for hh in range(HEADS)]


def _spatial_gating_groups(za_ref, ws_ref, bs_ref, yb_ref):
    n_chunks = za_ref.shape[0] // CHUNK

    def group(g):
        row = lax.broadcasted_iota(jnp.int32, (CHUNK, CHUNK), 0)
        col = lax.broadcasted_iota(jnp.int32, (CHUNK, CHUNK), 1)
        cols = lambda t: slice(t * TILE + g * HD, t * TILE + (g + 1) * HD)
        w = jnp.where(row >= col, ws_ref[g], 0.0).astype(BF16)
        bias = bs_ref[g]
        v_all = jnp.concatenate(
            [za_ref[c * CHUNK:(c + 1) * CHUNK, cols(T_SG_V)] for c in range(n_chunks)], axis=1)
        mixed_all = jnp.dot(w, v_all, preferred_element_type=F32)
        for c in range(n_chunks):
            rows = slice(c * CHUNK, (c + 1) * CHUNK)
            mixed = mixed_all[:, c * HD:(c + 1) * HD] + bias
            yb_ref[rows, g * HD:(g + 1) * HD] = (
                za_ref[rows, cols(T_SG_U)].astype(F32) * mixed).astype(BF16)

    return [lambda g=g: group(g) for g in range(HEADS)]


def _inproj(l, x, gmix, w, cos, sin, lng, lnb, qg, kg, decay, zeta, xi, gch, rg, ws, bs, wgu):
    s = x.shape[0]
    tm = TM_IN
    row = lambda i: (i, 0)
    n_steps = s // tm
    wgu_rows = D_MODEL // n_steps
    assert D_MODEL % n_steps == 0 and wgu_rows % 16 == 0
    table = _resident((HEADS, CHUNK, CHUNK))
    layer_table = _layer((HEADS, CHUNK, CHUNK), l)
    out_shape = [jax.ShapeDtypeStruct((s, BW), BF16)] * 2
    out_specs = [pl.BlockSpec((tm, BW), row)] * 2
    for _, d in DIL_PAIRS:
        out_shape.append(jax.ShapeDtypeStruct((3, d, s // d, BW), BF16))
        out_specs.append(pl.BlockSpec((3, d, tm // d, BW), lambda i: (0, 0, i, 0)))
    out_shape.append(jax.ShapeDtypeStruct((s, N_BRANCH * D_MODEL), BF16))
    out_specs.append(pl.BlockSpec((tm, N_BRANCH * D_MODEL), row))
    out_shape.append(jax.ShapeDtypeStruct((D_MODEL, 2 * D_FF), BF16))
    out_specs.append(pl.BlockSpec((wgu_rows, 2 * D_FF), row))
    return pl.pallas_call(
        _inproj_kernel,
        grid=(n_steps,),
        in_specs=[
            pl.BlockSpec((tm, D_MODEL), row),
            _layer((1, D_MODEL), l),
            _resident((D_MODEL, N_IN)),
            pl.BlockSpec((tm, HD // 2), row),
            pl.BlockSpec((tm, HD // 2), row),
            _layer((1, BW), l),
            _layer((1, BW), l),
            _layer((1, HD), l),
            _layer((1, HD), l),
            table, table, table, table,
            _layer((1, BW), l), layer_table, layer_table,
            pl.BlockSpec((None, wgu_rows, 2 * D_FF), lambda i: (l, i, 0)),
        ],
        out_specs=out_specs,
        out_shape=out_shape,
        scratch_shapes=[pltpu.VMEM((tm, D_MODEL), BF16),
                        pltpu.VMEM((tm, N_MIX_TILES * TILE), BF16),
                        pltpu.VMEM((2 * HEADS, SUB_IN, HD), F32),
                        pltpu.VMEM((HEADS, HD, HD), F32)],
        compiler_params=pltpu.CompilerParams(
            dimension_semantics=("arbitrary",), vmem_limit_bytes=VMEM_LIMIT_V7X),
        name="inproj",
    )(x, gmix, w, cos, sin, lng, lnb, qg, kg, decay, zeta, xi, gch, rg, ws, bs, wgu)


def _attn_kernel(n_cast, layer, bound_smem_ref, bound_ref, q_ref, kp_ref, kc_ref, vp_ref, vc_ref, *refs):
    w32_refs = refs[:n_cast]
    o_ref, lse_ref = refs[n_cast:n_cast + 2]
    w16_refs = refs[n_cast + 2:]
    for w32_ref, w16_ref in zip(w32_refs, w16_refs):
        w16_ref[...] = w32_ref[...].astype(BF16)

    n_res, tq = q_ref.shape[:2]
    first = pl.program_id(1) == 0
    row = lax.broadcasted_iota(jnp.int32, (CHUNK, 2 * CHUNK), 0)
    col = lax.broadcasted_iota(jnp.int32, (CHUNK, 2 * CHUNK), 1)
    band = (col >= row) & (col <= row + CHUNK)
    band_first = band & (col >= jnp.where(first, CHUNK, 0))
    lane = lax.broadcasted_iota(jnp.int32, (CHUNK, HD), 1)
    nt = (((1,), (1,)), ((), ()))
    bound = bound_ref[...]

    def operands(r, i, hh):
        rows = slice(i * CHUNK, (i + 1) * CHUNK)
        cols = slice(hh * HD, (hh + 1) * HD)
        if i == 0:
            k = jnp.concatenate([kp_ref[r, :, cols], kc_ref[r, rows, cols]], axis=0)
            v = jnp.concatenate([vp_ref[r, :, cols], vc_ref[r, rows, cols]], axis=0)
        else:
            keys = slice((i - 1) * CHUNK, (i + 1) * CHUNK)
            k = kc_ref[r, keys, cols]
            v = vc_ref[r, keys, cols]
        return rows, cols, q_ref[r, rows, cols], k, v

    blocks = [(r, i) for r in range(n_res) for i in range(tq // CHUNK)]
    bounded = bound_smem_ref[layer] <= BOUND_FAST_MAX

    @pl.when(bounded)
    def _():
        shift = jnp.where(band, -bound, NEG_INF)
        shift_first = jnp.where(band_first, -bound, NEG_INF)
        ones = jnp.ones((2 * CHUNK, HD), BF16)
        for r, i in blocks:
            den_tile = jnp.zeros((CHUNK, HD), F32)
            for hh in range(HEADS):
                rows, cols, q, k, v = operands(r, i, hh)
                s = lax.dot_general(q, k, nt, preferred_element_type=F32)
                p = jnp.exp2(s + (shift_first if i == 0 else shift)).astype(BF16)
                od = jnp.dot(p, jnp.concatenate([v, ones], axis=1), preferred_element_type=F32)
                den = od[:, HD:]
                o_ref[r, rows, cols] = (od[:, :HD] / den).astype(BF16)
                den_tile = jnp.where(lane // LSE_LANES == hh, den, den_tile)
            lse_ref[r, i * CHUNK:(i + 1) * CHUNK, :] = (bound[:, :HD] + jnp.log2(den_tile)) * LN2

    @pl.when(jnp.logical_not(bounded))
    def _():
        for r, i in blocks:
            lse_tile = jnp.zeros((CHUNK, HD), F32)
            for hh in range(HEADS):
                rows, cols, q, k, v = operands(r, i, hh)
                s = lax.dot_general(q, k, nt, preferred_element_type=F32)
                s = jnp.where(band_first if i == 0 else band, s, NEG_INF)
                m = jnp.max(s, axis=-1, keepdims=True)
                e = jnp.exp2(s - m)
                den = jnp.sum(e, axis=-1, keepdims=True)
                o = jnp.dot(e.astype(BF16), v, preferred_element_type=F32) / den
                o_ref[r, rows, cols] = o.astype(BF16)
                lse_tile = jnp.where(lane // LSE_LANES == hh, (m + jnp.log2(den)) * LN2, lse_tile)
            lse_ref[r, i * CHUNK:(i + 1) * CHUNK, :] = lse_tile


def _attention(l, a, d, bound, casts=()):
    length = a.shape[2]
    tq = min(TQ_ATT, length)
    n_res = min(d, TQ_ATT // tq)
    assert length % tq == 0 and d % n_res == 0
    per = tq // CHUNK
    nb = length // tq
    n_steps = d // n_res * nb
    cur = lambda c: pl.BlockSpec((None, n_res, tq, BW), lambda r, b, c=c: (c, r, b, 0))
    prev = lambda c: pl.BlockSpec((None, n_res, CHUNK, BW),
                                  lambda r, b, c=c: (c, r, jnp.maximum(b * per - 1, 0), 0))
    cast_in, cast_out, cast_shape = [], [], []
    for w, wl in casts:
        _, rows, cols = w.shape
        rt = rows // n_steps
        assert rows % n_steps == 0 and rt % 16 == 0
        cast_in.append(pl.BlockSpec((None, rt, cols), lambda r, b, wl=wl: (wl, r * nb + b, 0)))
        cast_out.append(pl.BlockSpec((rt, cols), lambda r, b: (r * nb + b, 0)))
        cast_shape.append(jax.ShapeDtypeStruct((rows, cols), BF16))
    return pl.pallas_call(
        functools.partial(_attn_kernel, len(casts), l),
        grid=(d // n_res, nb),
        in_specs=[pl.BlockSpec(memory_space=pltpu.SMEM), _layer((1, 2 * CHUNK), l),
                  cur(0), prev(1), cur(1), prev(2), cur(2)] + cast_in,
        out_specs=[pl.BlockSpec((n_res, tq, BW), lambda r, b: (r, b, 0)),
                   pl.BlockSpec((n_res, tq, HD), lambda r, b: (r, b, 0))] + cast_out,
        out_shape=[jax.ShapeDtypeStruct((d, length, BW), BF16),
                   jax.ShapeDtypeStruct((d, length, HD), F32)] + cast_shape,
        compiler_params=pltpu.CompilerParams(
            dimension_semantics=("arbitrary", "arbitrary"), vmem_limit_bytes=VMEM_LIMIT_V7X),
        name=f"attn_d{d}",
    )(bound[:, 0, 0], bound, a, a, a, a, a, *[w for w, _ in casts])


def _merge_ffn_kernel(x_ref, ya_ref, yb_ref, o0_ref, o1_ref, o2_ref, l0_ref, l1_ref, l2_ref, gt_ref,
                      wb_ref, wo_ref, gffn_ref, wgu_ref, wd_ref, out_ref, oscr_ref, lscr_ref, yc_ref):
    tm = x_ref.shape[0]

    def natural_order(src_ref, cols, scr_ref, slot, d):
        if d == 1:
            return src_ref[0, :, cols].astype(F32)
        for r in range(d):
            scr_ref[slot, pl.ds(r, tm // d, stride=d), :] = src_ref[r, :, cols].astype(F32)
        return scr_ref[slot]

    outs, lses = [], []
    for g, (o_ref, l_ref, (_, d)) in enumerate(zip((o0_ref, o1_ref, o2_ref), (l0_ref, l1_ref, l2_ref),
                                                   DIL_PAIRS)):
        outs.append([natural_order(o_ref, slice(hh * HD, (hh + 1) * HD), oscr_ref, g * HEADS + hh, d)
                     for hh in range(HEADS)])
        lses.append(natural_order(l_ref, slice(0, HD), lscr_ref, g, d))
    m = jnp.maximum(jnp.maximum(lses[0], lses[1]), lses[2])
    es = [jnp.exp(l - m) for l in lses]
    inv = 1.0 / (es[0] + es[1] + es[2])
    for hh in range(HEADS):
        cols = slice(hh * HD, (hh + 1) * HD)
        acc = jnp.zeros((tm, HD), F32)
        for g in range(N_DIL):
            wgt = (es[g] * inv)[:, hh * LSE_LANES:hh * LSE_LANES + 1]
            acc = acc + wgt * outs[g][hh]
        yc_ref[:, cols] = acc.astype(BF16)

    merged = jnp.zeros((tm, D_MODEL), F32)
    for b, y in enumerate((ya_ref[...], yb_ref[...], yc_ref[...])):
        gate = gt_ref[:, b * D_MODEL:(b + 1) * D_MODEL].astype(F32)
        merged = merged + gate * jnp.dot(y, wb_ref[b * BW:(b + 1) * BW, :], preferred_element_type=F32)
    x1 = x_ref[...] + jnp.dot(merged.astype(BF16), wo_ref[...], preferred_element_type=F32)

    ms = jnp.mean(x1 * x1, axis=-1, keepdims=True)
    h = (x1 * lax.rsqrt(ms + EPS) * gffn_ref[...]).astype(BF16)
    gate = jnp.dot(h, wgu_ref[:, :D_FF], preferred_element_type=F32)
    up = jnp.dot(h, wgu_ref[:, D_FF:], preferred_element_type=F32)
    act = (jax.nn.silu(gate) * up).astype(BF16)
    out_ref[...] = x1 + jnp.dot(act, wd_ref[...], preferred_element_type=F32)


def _merge_ffn(l, x, ya, yb, outs, lses, gates, wb, wo, gffn, wgu, wd):
    s = x.shape[0]
    tm = TM_OUT
    row = lambda i: (i, 0)
    sub = lambda i: (0, i, 0)
    o_specs = [pl.BlockSpec((d, tm // d, BW), sub) for _, d in DIL_PAIRS]
    l_specs = [pl.BlockSpec((d, tm // d, HD), sub) for _, d in DIL_PAIRS]
    return pl.pallas_call(
        _merge_ffn_kernel,
        grid=(s // tm,),
        in_specs=[pl.BlockSpec((tm, D_MODEL), row),
                  pl.BlockSpec((tm, BW), row), pl.BlockSpec((tm, BW), row)]
                 + o_specs + l_specs
                 + [pl.BlockSpec((tm, N_BRANCH * D_MODEL), row),
                    _resident((N_BRANCH * BW, D_MODEL)),
                    _resident((D_MODEL, D_MODEL)),
                    _layer((1, D_MODEL), l),
                    _resident((D_MODEL, 2 * D_FF)),
                    _resident((D_FF, D_MODEL))],
        out_specs=pl.BlockSpec((tm, D_MODEL), row),
        out_shape=jax.ShapeDtypeStruct((s, D_MODEL), F32),
        scratch_shapes=[pltpu.VMEM((N_DIL * HEADS, tm, HD), F32), pltpu.VMEM((N_DIL, tm, HD), F32),
                        pltpu.VMEM((tm, BW), BF16)],
        compiler_params=pltpu.CompilerParams(
            dimension_semantics=("arbitrary",), vmem_limit_bytes=VMEM_LIMIT_V7X),
        name="merge_ffn",
    )(x, ya, yb, *outs, *lses, gates, wb, wo, gffn, wgu, wd)


def _retention_tables():
    c = CHUNK
    log_g = jnp.log1p(-(2.0 ** (-5.0 - jnp.arange(HEADS, dtype=F32))))
    n = jnp.arange(c, dtype=F32)
    diff = n[:, None] - n[None, :]
    scale = HD ** -0.5
    decay = jnp.where(diff >= 0, jnp.exp(log_g[:, None, None] * jnp.maximum(diff, 0.0)), 0.0) * scale
    zeta = jnp.exp(log_g[:, None] * (c - 1.0 - n)[None, :]) * scale
    xi = jnp.exp(log_g[:, None] * (n + 1.0)[None, :])
    g_chunk = jnp.exp(log_g * c)
    full = lambda t: jnp.broadcast_to(t, (HEADS, c, c)).astype(F32)
    return decay, full(zeta[:, :, None]), full(xi[:, :, None]), full(g_chunk[:, None, None])


def _rotary_tables(s):
    half = HD // 2
    inv = 1.0 / (ROPE_BASE ** (jnp.arange(half, dtype=F32) / half))
    ang = jnp.arange(s, dtype=jnp.int32).astype(F32)[:, None] * inv[None, :]
    return jnp.cos(ang), jnp.sin(ang)


def kernel(x, g_mix, w_in, ret_norm_g, sg_ln_g, sg_ln_b, sg_w, sg_b, dil_q_norm_g, dil_k_norm_g,
           w_branch, w_o, g_ffn, w_gate_up, w_down):
    batch, s, d_model = x.shape
    depth = w_in.shape[0]
    assert batch == 1 and d_model == D_MODEL and w_in.shape[-1] == N_IN
    assert s % (DIL_PAIRS[-1][1] * CHUNK) == 0 and s % TM_IN == 0 and s % TM_OUT == 0
    for win, d in DIL_PAIRS:
        assert win // d == CHUNK and (s // d) % min(TQ_ATT, s // d) == 0
        assert SUB_IN % (16 * d) == 0 and TM_IN % SUB_IN == 0 and TM_OUT % (16 * d) == 0

    w_in_l = w_in[0].astype(BF16)
    w_branch_rows = w_branch.reshape(depth, N_BRANCH * BW, D_MODEL)
    cos, sin = _rotary_tables(s)
    decay, zeta, xi, gch = _retention_tables()
    bias_s = jnp.broadcast_to(sg_b[:, :, :, None], sg_b.shape + (CHUNK,)).astype(F32)
    vec = lambda p: p[:, None, :]
    score_bound = (LOG2E * HD ** 0.5) * (jnp.max(jnp.abs(dil_q_norm_g), axis=-1)
                                         * jnp.max(jnp.abs(dil_k_norm_g), axis=-1))
    score_bound = jnp.broadcast_to(score_bound[:, None, None], (depth, 1, 2 * CHUNK)).astype(F32)

    xs = x.reshape(s, d_model)
    for l in range(depth):
        ya, yb, a0, a1, a2, gates, w_gu_l = _inproj(
            l, xs, vec(g_mix), w_in_l, cos, sin, vec(sg_ln_g), vec(sg_ln_b),
            vec(dil_q_norm_g), vec(dil_k_norm_g), decay, zeta, xi, gch, vec(ret_norm_g), sg_w, bias_s,
            w_gate_up)
        casts = ([(w_down, l)],
                 [(w_branch_rows, l), (w_o, l)],
                 [(w_in, l + 1)] if l + 1 < depth else [])
        outs, lses, w16 = [], [], []
        for a, (_, d), cast in zip((a0, a1, a2), DIL_PAIRS, casts):
            o, lse, *copies = _attention(l, a, d, score_bound, cast)
            outs.append(o)
            lses.append(lse)
            w16 += copies
        w_d_l, w_b_l, w_o_l = w16[:3]
        xs = _merge_ffn(l, xs, ya, yb, outs, lses, gates, w_b_l, w_o_l, vec(g_ffn), w_gu_l, w_d_l)
        if l + 1 < depth:
            w_in_l = w16[3]
    return xs.reshape(batch, s, d_model)
```
